```python
import math
import jax
import jax.numpy as jnp
from jax import lax
import numpy as np


D_MODEL = 1024
BATCH = 8
SEQ = 4096
DEPTH = 1

SSM_WIDTH = 512
SSM_GROUP = 16
SSM_GROUPS = SSM_WIDTH // SSM_GROUP
SSM_STATE = 64
NSA_HEADS = 8
NSA_KV_HEADS = 2
NSA_REP = NSA_HEADS // NSA_KV_HEADS
HEAD_DIM = 64
NSA_WIDTH = NSA_HEADS * HEAD_DIM
KV_WIDTH = NSA_KV_HEADS * HEAD_DIM
CMP_BLOCK = 32
CMP_STRIDE = 16
CMP_HIDDEN = 128
SEL_BLOCK = 64
N_SELECT = 16
WINDOW = 512
Q_BLOCK = 128
N_BRANCH = 2
D_FF = 2816
CONV_WIDTH = 3
PLE_DIM = 256
EPS = 1e-6
NEG = -1e30
FORCE_BONUS = 100.0

OFF_SSM = 0
OFF_Q = OFF_SSM + SSM_WIDTH
OFF_KV = OFF_Q + NSA_WIDTH
OFF_NSA_GATE = OFF_KV + 6 * KV_WIDTH
OFF_MERGE = OFF_NSA_GATE + 3 * NSA_HEADS
IN_WIDTH = OFF_MERGE + N_BRANCH * D_MODEL

kernel_name = 'hybrid_s5_nsa_convffn_ple'


def rmsnorm(x, g):
    xf = x.astype(jnp.float32)
    y = xf * lax.rsqrt(jnp.mean(xf * xf, axis=-1, keepdims=True) + EPS)
    return (y * g.astype(jnp.float32)).astype(x.dtype)


def s5_mixer(u, a_re, a_im, log_dt, b_re, b_im, c_re, c_im, d_skip, w_glu):
    f32 = jnp.float32
    bsz, t, _ = u.shape
    uf = u.astype(f32)
    ug = uf.reshape(bsz, t, SSM_GROUPS, SSM_GROUP)
    lam = lax.complex(jnp.minimum(a_re.astype(f32), -1e-4), a_im.astype(f32))
    dt = jnp.exp(log_dt.astype(f32))[:, None]
    a_bar = jnp.exp(lam * dt)
    b_bar = lax.complex(b_re.astype(f32), b_im.astype(f32)) * ((a_bar - 1.0) / lam)[:, :, None]
    bu = lax.complex(jnp.einsum('btgc,gpc->btgp', ug, jnp.real(b_bar)),
                     jnp.einsum('btgc,gpc->btgp', ug, jnp.imag(b_bar)))
    a_seq = jnp.broadcast_to(a_bar, bu.shape)

    def combine(left, right):
        a_l, b_l = left
        a_r, b_r = right
        return a_r * a_l, a_r * b_l + b_r

    _, states = lax.associative_scan(combine, (a_seq, bu), axis=1)
    y = (jnp.einsum('gcp,btgp->btgc', c_re.astype(f32), jnp.real(states))
         - jnp.einsum('gcp,btgp->btgc', c_im.astype(f32), jnp.imag(states)))
    y = y.reshape(bsz, t, SSM_WIDTH) + d_skip.astype(f32) * uf
    y = jax.nn.gelu(y)
    y = y * jax.nn.sigmoid(y @ w_glu.astype(f32))
    return y.astype(u.dtype)


def compress_blocks(k, pe, w1, w2):
    bsz, g, t, dh = k.shape
    ch = k.reshape(bsz, g, t // CMP_STRIDE, CMP_STRIDE, dh)
    blocks = jnp.concatenate([ch[:, :, :-1], ch[:, :, 1:]], axis=3) + pe
    flat = blocks.reshape(bsz, g, blocks.shape[2], CMP_BLOCK * dh)
    return jax.nn.gelu(flat @ w1) @ w2


def nsa_mixer(q, kv, gate_logits, pe_k, pe_v, wk1, wk2, wv1, wv2):
    f32 = jnp.float32
    bsz, t, _ = q.shape
    G, R, dh = NSA_KV_HEADS, NSA_REP, HEAD_DIM
    nb = t // Q_BLOCK
    n_cmp = t // CMP_STRIDE - 1
    n_slc = t // SEL_BLOCK
    n_sel = min(N_SELECT, n_slc)
    qh = (q * (dh ** -0.5)).reshape(bsz, t, G, R, dh).transpose(0, 2, 3, 1, 4)
    kvh = kv.reshape(bsz, t, 6, G, dh).transpose(2, 0, 3, 1, 4)
    k_cmp, v_cmp, k_sel, v_sel, k_win, v_win = kvh[0], kvh[1], kvh[2], kvh[3], kvh[4], kvh[5]
    pos = jnp.arange(t)

    kc = compress_blocks(k_cmp, pe_k, wk1, wk2)
    vc = compress_blocks(v_cmp, pe_v, wv1, wv2)
    cmp_end = jnp.arange(n_cmp) * CMP_STRIDE + CMP_BLOCK - 1
    cmask = cmp_end[None, :] <= pos[:, None]
    s = jnp.einsum('bgrtd,bgnd->bgrtn', qh, kc).astype(f32)
    p_cmp = jnp.where(cmask, jax.nn.softmax(jnp.where(cmask, s, NEG), axis=-1), 0.0)
    o_cmp = jnp.einsum('bgrtn,bgnd->bgrtd', p_cmp.astype(vc.dtype), vc)

    starts = np.arange(n_cmp) * CMP_STRIDE
    sel_starts = np.arange(n_slc) * SEL_BLOCK
    overlap = jnp.asarray(((starts[:, None] < sel_starts[None, :] + SEL_BLOCK)
                           & (starts[:, None] + CMP_BLOCK > sel_starts[None, :])).astype(np.float32))
    imp = jnp.einsum('bgrtn,nj->bgtj', p_cmp, overlap)
    cur = pos // SEL_BLOCK
    blk = jnp.arange(n_slc)
    valid = blk[None, :] <= cur[:, None]
    forced = (blk[None, :] == 0) | (blk[None, :] == cur[:, None]) | (blk[None, :] == cur[:, None] - 1)
    score = jnp.where(valid, imp, -1.0) + jnp.where(forced, FORCE_BONUS, 0.0)
    _, sel_idx = lax.top_k(score, n_sel)

    n_keys = n_sel * SEL_BLOCK
    bi = jnp.arange(bsz)[:, None, None]
    gi = jnp.arange(G)[None, :, None]
    offs = jnp.arange(SEL_BLOCK)

    def sel_block(args):
        qb, ib, tb = args
        tok = (ib[..., None] * SEL_BLOCK + offs).reshape(bsz, G, Q_BLOCK * n_keys)
        kg = k_sel[bi, gi, tok].reshape(bsz, G, Q_BLOCK, n_keys, dh)
        vg = v_sel[bi, gi, tok].reshape(bsz, G, Q_BLOCK, n_keys, dh)
        m = tok.reshape(bsz, G, Q_BLOCK, n_keys) <= tb[None, None, :, None]
        sb = jnp.einsum('bgrqd,bgqkd->bgrqk', qb, kg).astype(f32)
        pb = jax.nn.softmax(jnp.where(m[:, :, None], sb, NEG), axis=-1)
        return jnp.einsum('bgrqk,bgqkd->bgrqd', pb.astype(vg.dtype), vg)

    q_blocks = qh.reshape(bsz, G, R, nb, Q_BLOCK, dh).transpose(3, 0, 1, 2, 4, 5)
    i_blocks = sel_idx.reshape(bsz, G, nb, Q_BLOCK, n_sel).transpose(2, 0, 1, 3, 4)
    t_blocks = pos.reshape(nb, Q_BLOCK)
    o_sel = lax.map(sel_block, (q_blocks, i_blocks, t_blocks))
    o_sel = o_sel.transpose(1, 2, 3, 0, 4, 5).reshape(bsz, G, R, t, dh)

    nw = WINDOW // Q_BLOCK

    def band(z):
        zb = jnp.pad(z.reshape(bsz, G, nb, Q_BLOCK, dh), ((0, 0), (0, 0), (nw, 0), (0, 0), (0, 0)))
        return jnp.concatenate([zb[:, :, j:j + nb] for j in range(nw + 1)], axis=3)

    kw = band(k_win)
    vw = band(v_win)
    kpos = ((jnp.arange(nb)[:, None] - nw) * Q_BLOCK + jnp.arange((nw + 1) * Q_BLOCK)[None, :])[:, None, :]
    qpos = t_blocks[:, :, None]
    wmask = (kpos >= 0) & (kpos <= qpos) & (qpos - kpos < WINDOW)
    qw = qh.reshape(bsz, G, R, nb, Q_BLOCK, dh)
    sw = jnp.einsum('bgrnqd,bgnkd->bgrnqk', qw, kw).astype(f32)
    pw = jax.nn.softmax(jnp.where(wmask, sw, NEG), axis=-1)
    o_win = jnp.einsum('bgrnqk,bgnkd->bgrnqd', pw.astype(vw.dtype), vw).reshape(bsz, G, R, t, dh)

    gates = jax.nn.sigmoid(gate_logits.astype(f32)).reshape(bsz, t, 3, G, R).transpose(2, 0, 3, 4, 1)[..., None]
    o = gates[0] * o_cmp + gates[1] * o_sel + gates[2] * o_win
    return o.transpose(0, 3, 1, 2, 4).reshape(bsz, t, NSA_WIDTH).astype(q.dtype)


def conv_ffn(x, w_up, conv_w, conv_b, w_down):
    hid = x @ w_up
    hid = lax.conv_general_dilated(hid, conv_w[:, None, :], window_strides=(1,),
                                   padding=[(CONV_WIDTH - 1, 0)],
                                   dimension_numbers=('NWC', 'WIO', 'NWC'),
                                   feature_group_count=hid.shape[-1]) + conv_b
    a, b = jnp.split(hid, 2, axis=-1)
    return (jax.nn.gelu(a) * b) @ w_down


def setup_inputs(seed: int = 0) -> dict:
    key = jax.random.key(seed)
    ks = jax.random.split(key, 40)
    f32 = jnp.float32
    L = DEPTH
    G, P, C = SSM_GROUPS, SSM_STATE, SSM_GROUP

    def nrm(k, shape, scale):
        return jax.random.normal(k, shape, f32) * scale

    return {
        'x': nrm(ks[0], (BATCH, SEQ, D_MODEL), 1.0),
        'p': nrm(ks[1], (DEPTH, BATCH, SEQ, PLE_DIM), 1.0),
        'g_mix': 1.0 + nrm(ks[2], (L, D_MODEL), 0.02),
        'w_in': nrm(ks[3], (L, D_MODEL, IN_WIDTH), D_MODEL ** -0.5),
        'ssm_a_re': -0.5 + nrm(ks[4], (L, G, P), 0.01),
        'ssm_a_im': math.pi * jnp.arange(P, dtype=f32) + nrm(ks[5], (L, G, P), 0.01),
        'ssm_log_dt': jax.random.uniform(ks[6], (L, G), f32, math.log(1e-3), math.log(1e-1)),
        'ssm_b_re': nrm(ks[7], (L, G, P, C), (2 * C) ** -0.5),
        'ssm_b_im': nrm(ks[8], (L, G, P, C), (2 * C) ** -0.5),
        'ssm_c_re': nrm(ks[9], (L, G, C, P), P ** -0.5),
        'ssm_c_im': nrm(ks[10], (L, G, C, P), P ** -0.5),
        'ssm_d': nrm(ks[11], (L, SSM_WIDTH), 1.0),
        'ssm_w_glu': nrm(ks[12], (L, SSM_WIDTH, SSM_WIDTH), SSM_WIDTH ** -0.5),
        'cmp_pe_k': nrm(ks[13], (L, CMP_BLOCK, HEAD_DIM), 0.02),
        'cmp_pe_v': nrm(ks[14], (L, CMP_BLOCK, HEAD_DIM), 0.02),
        'cmp_wk1': nrm(ks[15], (L, CMP_BLOCK * HEAD_DIM, CMP_HIDDEN), (CMP_BLOCK * HEAD_DIM) ** -0.5),
        'cmp_wk2': nrm(ks[16], (L, CMP_HIDDEN, HEAD_DIM), CMP_HIDDEN ** -0.5),
        'cmp_wv1': nrm(ks[17], (L, CMP_BLOCK * HEAD_DIM, CMP_HIDDEN), (CMP_BLOCK * HEAD_DIM) ** -0.5),
        'cmp_wv2': nrm(ks[18], (L, CMP_HIDDEN, HEAD_DIM), CMP_HIDDEN ** -0.5),
        'w_br_ssm': nrm(ks[19], (L, SSM_WIDTH, D_MODEL), SSM_WIDTH ** -0.5),
        'w_br_nsa': nrm(ks[20], (L, NSA_WIDTH, D_MODEL), NSA_WIDTH ** -0.5),
        'w_out': nrm(ks[21], (L, D_MODEL, D_MODEL), D_MODEL ** -0.5),
        'g_ffn': 1.0 + nrm(ks[22], (L, D_MODEL), 0.02),
        'w_up': nrm(ks[23], (L, D_MODEL, 2 * D_FF), D_MODEL ** -0.5),
        'conv_w': nrm(ks[24], (L, CONV_WIDTH, 2 * D_FF), CONV_WIDTH ** -0.5),
        'conv_b': nrm(ks[25], (L, 2 * D_FF), 0.01),
        'w_down': nrm(ks[26], (L, D_FF, D_MODEL), D_FF ** -0.5),
        'g_ple': 1.0 + nrm(ks[27], (L, D_MODEL), 0.02),
        'w_ple_gate': nrm(ks[28], (L, D_MODEL, D_MODEL), D_MODEL ** -0.5),
        'w_ple_proj': nrm(ks[29], (L, PLE_DIM, D_MODEL), PLE_DIM ** -0.5),
        'g_final': 1.0 + nrm(ks[30], (D_MODEL,), 0.02),
    }


def reference(x, p, g_mix, w_in, ssm_a_re, ssm_a_im, ssm_log_dt, ssm_b_re, ssm_b_im, ssm_c_re, ssm_c_im,
              ssm_d, ssm_w_glu, cmp_pe_k, cmp_pe_v, cmp_wk1, cmp_wk2, cmp_wv1, cmp_wv2, w_br_ssm, w_br_nsa,
              w_out, g_ffn, w_up, conv_w, conv_b, w_down, g_ple, w_ple_gate, w_ple_proj, g_final):
    h = x
    for i in range(DEPTH):
        n1 = rmsnorm(h, g_mix[i])
        z = n1 @ w_in[i]
        y_ssm = s5_mixer(z[..., OFF_SSM:OFF_Q], ssm_a_re[i], ssm_a_im[i], ssm_log_dt[i], ssm_b_re[i], ssm_b_im[i],
                         ssm_c_re[i], ssm_c_im[i], ssm_d[i], ssm_w_glu[i])
        y_nsa = nsa_mixer(z[..., OFF_Q:OFF_KV], z[..., OFF_KV:OFF_NSA_GATE], z[..., OFF_NSA_GATE:OFF_MERGE],
                          cmp_pe_k[i], cmp_pe_v[i], cmp_wk1[i], cmp_wk2[i], cmp_wv1[i], cmp_wv2[i])
        merge = jax.nn.sigmoid(z[..., OFF_MERGE:].astype(jnp.float32))
        mixed = merge[..., :D_MODEL] * (y_ssm @ w_br_ssm[i]) + merge[..., D_MODEL:] * (y_nsa @ w_br_nsa[i])
        h = h + mixed.astype(h.dtype) @ w_out[i]
        h = h + conv_ffn(rmsnorm(h, g_ffn[i]), w_up[i], conv_w[i], conv_b[i], w_down[i])
        gate = jax.nn.sigmoid(rmsnorm(h, g_ple[i]) @ w_ple_gate[i])
        h = h + gate * (p[i] @ w_ple_proj[i])
    return rmsnorm(h, g_final)
```

```python
import functools

import numpy as np
import jax
import jax.numpy as jnp
from jax import lax
from jax.experimental import pallas as pl
from jax.experimental.pallas import tpu as pltpu

F32 = jnp.float32
BF16 = jnp.bfloat16

D_MODEL = 1024
SSM_WIDTH = 512
SSM_GROUP = 16
SSM_GROUPS = SSM_WIDTH // SSM_GROUP
SSM_STATE = 64
NSA_HEADS = 8
NSA_KV_HEADS = 2
NSA_REP = NSA_HEADS // NSA_KV_HEADS
HEAD_DIM = 64
NSA_WIDTH = NSA_HEADS * HEAD_DIM
KV_WIDTH = NSA_KV_HEADS * HEAD_DIM
CMP_BLOCK = 32
CMP_STRIDE = 16
CMP_HIDDEN = 128
SEL_BLOCK = 64
N_SELECT = 16
WINDOW = 512
D_FF = 2816
PLE_DIM = 256
EPS = 1e-6
NEG = -1e30
FORCE_BONUS = 100.0

OFF_Q = SSM_WIDTH
OFF_KV = OFF_Q + NSA_WIDTH
OFF_NSA_GATE = OFF_KV + 6 * KV_WIDTH
OFF_MERGE = OFF_NSA_GATE + 3 * NSA_HEADS

LANES = 128
SUBLANES = 8
VMEM_LIMIT = 56 * 1024 * 1024
GATE_PAD = LANES
SEL_PAD = 64
V_ROWS = 80
SEL_BIAS = 2.0 ** 100
SSM_BLOCKS = SSM_WIDTH // LANES
SSM_BLOCK_STATE = (LANES // SSM_GROUP) * SSM_STATE
FF_CHUNK = 256
N_FF_CHUNKS = D_FF // FF_CHUNK


def _rms(x, g):
    return x * lax.rsqrt(jnp.mean(x * x, axis=-1, keepdims=True) + EPS) * g


def _dot(a, b):
    return jnp.dot(a, b, preferred_element_type=F32)


def _params(*sem):
    return pltpu.CompilerParams(dimension_semantics=sem, vmem_limit_bytes=VMEM_LIMIT)


def _const_spec(shape):
    zeros = (0,) * len(shape)
    return pl.BlockSpec(shape, lambda *_: zeros, pipeline_mode=pl.Buffered(1))


def _inproj_kernel(x_ref, g_ref, w_ref, u_ref, q_ref, kv_ref, gl_ref):
    n1 = _rms(x_ref[...], g_ref[...]).astype(BF16)
    z = _dot(n1, w_ref[...])
    u_ref[...] = z[:, :OFF_Q]
    q_ref[...] = z[:, OFF_Q:OFF_KV]
    kv_ref[...] = z[:, OFF_KV:OFF_NSA_GATE]
    gl_ref[...] = z[:, OFF_NSA_GATE:]


def _inproj(x, g, w, tm):
    bsz, t, d = x.shape
    n = w.shape[1]
    return pl.pallas_call(
        _inproj_kernel,
        grid=(bsz, t // tm),
        in_specs=[
            pl.BlockSpec((None, tm, d), lambda b, i: (b, i, 0)),
            _const_spec((1, d)),
            _const_spec((d, n)),
        ],
        out_specs=[
            pl.BlockSpec((tm, SSM_WIDTH), lambda b, i: (i, b)),
            pl.BlockSpec((None, tm, NSA_WIDTH), lambda b, i: (b, i, 0)),
            pl.BlockSpec((None, tm, 6 * KV_WIDTH), lambda b, i: (b, i, 0)),
            pl.BlockSpec((None, tm, GATE_PAD), lambda b, i: (b, i, 0)),
        ],
        out_shape=[
            jax.ShapeDtypeStruct((t, bsz * SSM_WIDTH), F32),
            jax.ShapeDtypeStruct((bsz, t, NSA_WIDTH), F32),
            jax.ShapeDtypeStruct((bsz, t, 6 * KV_WIDTH), F32),
            jax.ShapeDtypeStruct((bsz, t, GATE_PAD), F32),
        ],
        compiler_params=_params("parallel", "parallel"),
        name="inproj",
    )(x, g, w)


def _ssm_kernel(u_ref, wb_ref, wc_ref, ar_ref, ai_ref, d_ref, wglu_ref, o_ref, bu_ref, st_ref, *, tc, nb):
    @pl.when(pl.program_id(0) == 0)
    def _():
        st_ref[...] = jnp.zeros_like(st_ref)

    u = u_ref[...]
    ub = u.astype(BF16)
    w2 = 2 * SSM_BLOCK_STATE
    for cb in range(SSM_BLOCKS):
        bu_ref[:, cb * w2:(cb + 1) * w2] = _dot(ub[:, cb * LANES:(cb + 1) * LANES], wb_ref[cb])

    for cb in range(SSM_BLOCKS):
        re_cols = slice(cb * w2, cb * w2 + SSM_BLOCK_STATE)
        im_cols = slice(cb * w2 + SSM_BLOCK_STATE, (cb + 1) * w2)
        ar = jnp.broadcast_to(ar_ref[cb], (nb, SSM_BLOCK_STATE))
        ai = jnp.broadcast_to(ai_ref[cb], (nb, SSM_BLOCK_STATE))

        def step(t, carry, re_cols=re_cols, im_cols=im_cols, ar=ar, ai=ai):
            sr, si = carry
            rows = pl.ds(pl.multiple_of(t * nb, nb), nb)
            nr = ar * sr - ai * si + bu_ref[rows, re_cols]
            ni = ar * si + ai * sr + bu_ref[rows, im_cols]
            bu_ref[rows, re_cols] = nr
            bu_ref[rows, im_cols] = ni
            return nr, ni

        sr, si = lax.fori_loop(0, tc, step, (st_ref[:, re_cols], st_ref[:, im_cols]), unroll=8)
        st_ref[:, re_cols] = sr
        st_ref[:, im_cols] = si

    ys = [_dot(bu_ref[:, cb * w2:(cb + 1) * w2].astype(BF16), wc_ref[cb]) for cb in range(SSM_BLOCKS)]
    y = jax.nn.gelu(jnp.concatenate(ys, axis=1) + d_ref[...] * u)
    gate = _dot(y.astype(BF16), wglu_ref[...])
    o_ref[...] = (y * jax.nn.sigmoid(gate)).astype(o_ref.dtype)


def _ssm_weights(a_re, a_im, log_dt, b_re, b_im, c_re, c_im):
    gpb = LANES // SSM_GROUP
    dt = jnp.exp(log_dt)[:, None]
    lr = jnp.minimum(a_re, -1e-4)
    li = a_im
    mag = jnp.exp(lr * dt)
    abr = mag * jnp.cos(li * dt)
    abi = mag * jnp.sin(li * dt)
    den = lr * lr + li * li
    qr = ((abr - 1.0) * lr + abi * li) / den
    qi = (abi * lr - (abr - 1.0) * li) / den
    bbr = b_re * qr[:, :, None] - b_im * qi[:, :, None]
    bbi = b_re * qi[:, :, None] + b_im * qr[:, :, None]
    eye = jnp.eye(gpb, dtype=F32)

    def pack_b(m):
        m4 = m.reshape(SSM_BLOCKS, gpb, SSM_STATE, SSM_GROUP)
        return jnp.einsum("kgpc,gh->kgchp", m4, eye).reshape(SSM_BLOCKS, LANES, SSM_BLOCK_STATE)

    def pack_c(m):
        m4 = m.reshape(SSM_BLOCKS, gpb, SSM_GROUP, SSM_STATE)
        return jnp.einsum("kgcp,gh->kgphc", m4, eye).reshape(SSM_BLOCKS, SSM_BLOCK_STATE, LANES)

    wb = jnp.concatenate([pack_b(bbr), pack_b(bbi)], axis=2).astype(BF16)
    wc = jnp.concatenate([pack_c(c_re), -pack_c(c_im)], axis=1).astype(BF16)
    ar = abr.reshape(SSM_BLOCKS, 1, SSM_BLOCK_STATE)
    ai = abi.reshape(SSM_BLOCKS, 1, SSM_BLOCK_STATE)
    return wb, wc, ar, ai


def _ssm(u_tb, wb, wc, ar, ai, d, wglu, nb, tc):
    rows = u_tb.shape[0]
    blk = tc * nb
    w2 = 2 * SSM_BLOCK_STATE
    return pl.pallas_call(
        functools.partial(_ssm_kernel, tc=tc, nb=nb),
        grid=(rows // blk,),
        in_specs=[
            pl.BlockSpec((blk, SSM_WIDTH), lambda i: (i, 0)),
            _const_spec(wb.shape),
            _const_spec(wc.shape),
            _const_spec(ar.shape),
            _const_spec(ai.shape),
            _const_spec(d.shape),
            _const_spec(wglu.shape),
        ],
        out_specs=pl.BlockSpec((blk, SSM_WIDTH), lambda i: (i, 0)),
        out_shape=jax.ShapeDtypeStruct((rows, SSM_WIDTH), BF16),
        scratch_shapes=[
            pltpu.VMEM((blk, SSM_BLOCKS * w2), F32),
            pltpu.VMEM((nb, SSM_BLOCKS * w2), F32),
        ],
        compiler_params=_params("arbitrary"),
        name="ssm",
    )(u_tb, wb, wc, ar, ai, d, wglu)


def _cmp_kernel(x_ref, pe_ref, w1_ref, w2_ref, o_ref, *, transpose_out):
    x = x_ref[...]
    n = x.shape[0]
    first = _dot((x + pe_ref[0]).astype(BF16), w1_ref[0])
    second = _dot((x + pe_ref[1]).astype(BF16), w1_ref[1])
    hid = first + pltpu.roll(second, n - 1, 0)
    hb = jax.nn.gelu(hid).astype(BF16)
    if transpose_out:
        out = lax.dot_general(w2_ref[...], hb, (((1,), (1,)), ((), ())), preferred_element_type=F32)
    else:
        out = _dot(hb, w2_ref[...])
    o_ref[...] = out.astype(o_ref.dtype)


def _compress(chunks, pe, w1, w2, transpose_out):
    bsz, g, n, width = chunks.shape
    pe2 = pe.reshape(2, 1, width)
    w12 = w1.reshape(2, width, CMP_HIDDEN).astype(BF16)
    if transpose_out:
        w2k = w2.T.astype(BF16)
        out_block, out_shape = (None, None, HEAD_DIM, n), (bsz, g, HEAD_DIM, n)
    else:
        w2k = w2.astype(BF16)
        out_block, out_shape = (None, None, n, HEAD_DIM), (bsz, g, n, HEAD_DIM)
    return pl.pallas_call(
        functools.partial(_cmp_kernel, transpose_out=transpose_out),
        grid=(bsz, g),
        in_specs=[
            pl.BlockSpec((None, None, n, width), lambda b, h: (b, h, 0, 0)),
            _const_spec(pe2.shape),
            _const_spec(w12.shape),
            _const_spec(w2k.shape),
        ],
        out_specs=pl.BlockSpec(out_block, lambda b, h: (b, h, 0, 0)),
        out_shape=jax.ShapeDtypeStruct(out_shape, BF16),
        compiler_params=_params("parallel", "parallel"),
        name="compress_v" if transpose_out else "compress_k",
    )(chunks, pe2, w12, w2k)


def _cmpattn_kernel(q_ref, kc_ref, vct_ref, ov_ref, qt_ref, oc_ref, *, tq, n_slc, n_sel):
    t0 = pl.program_id(2) * tq
    qt = (q_ref[...] * (HEAD_DIM ** -0.5)).T.astype(BF16)
    kc = kc_ref[...]
    vct = vct_ref[...]
    ncp = kc.shape[0]
    n_idx = lax.broadcasted_iota(jnp.int32, (ncp, tq), 0)
    t_idx = t0 + lax.broadcasted_iota(jnp.int32, (ncp, tq), 1)
    cmask = n_idx * CMP_STRIDE + (CMP_BLOCK - 1) <= t_idx
    probs = []
    for r in range(NSA_REP):
        qr = qt[r * HEAD_DIM:(r + 1) * HEAD_DIM, :]
        s = jnp.where(cmask, _dot(kc, qr), NEG)
        e = jnp.exp(s - jnp.max(s, axis=0, keepdims=True))
        p = jnp.where(cmask, e / jnp.sum(e, axis=0, keepdims=True), 0.0).astype(BF16)
        probs.append(p)
        oc_ref[:, r * tq:(r + 1) * tq] = _dot(vct, p)
        qt_ref[0:HEAD_DIM, r * tq:(r + 1) * tq] = qr

    imp = _dot(ov_ref[...], jnp.concatenate(probs, axis=0))
    j_idx = lax.broadcasted_iota(jnp.int32, (SEL_PAD, tq), 0)
    cur = (t0 + lax.broadcasted_iota(jnp.int32, (SEL_PAD, tq), 1)) // SEL_BLOCK
    valid = j_idx <= cur
    forced = (j_idx == 0) | (j_idx == cur) | (j_idx == cur - 1)
    score = jnp.where(valid, imp, -1.0) + jnp.where(forced, FORCE_BONUS, 0.0)
    rank = jnp.zeros((SEL_PAD, tq), jnp.int32)
    for j in range(n_slc):
        row = jnp.broadcast_to(score[j:j + 1, :], (SEL_PAD, tq))
        beats = (row > score) | ((row == score) & (j_idx > j))
        rank = rank + beats.astype(jnp.int32)
    sel = (rank < n_sel) & valid
    selm1 = jnp.where(sel, 0.0, -1.0).astype(BF16)
    for r in range(NSA_REP):
        qt_ref[HEAD_DIM:HEAD_DIM + SEL_PAD, r * tq:(r + 1) * tq] = selm1


def _cmpattn(q, kc, vct, ovt, tq):
    bsz, t, _ = q.shape
    g = NSA_KV_HEADS
    ncp = kc.shape[2]
    n_slc = t // SEL_BLOCK
    kern = functools.partial(_cmpattn_kernel, tq=tq, n_slc=n_slc, n_sel=min(N_SELECT, n_slc))
    return pl.pallas_call(
        kern,
        grid=(bsz, g, t // tq),
        in_specs=[
            pl.BlockSpec((None, tq, NSA_REP * HEAD_DIM), lambda b, h, i: (b, i, h)),
            pl.BlockSpec((None, None, ncp, HEAD_DIM), lambda b, h, i: (b, h, 0, 0)),
            pl.BlockSpec((None, None, HEAD_DIM, ncp), lambda b, h, i: (b, h, 0, 0)),
            _const_spec(ovt.shape),
        ],
        out_specs=[
            pl.BlockSpec((None, None, HEAD_DIM + SEL_PAD, NSA_REP * tq), lambda b, h, i: (b, h, 0, i)),
            pl.BlockSpec((None, None, HEAD_DIM, NSA_REP * tq), lambda b, h, i: (b, h, 0, i)),
        ],
        out_shape=[
            jax.ShapeDtypeStruct((bsz, g, HEAD_DIM + SEL_PAD, NSA_REP * t), BF16),
            jax.ShapeDtypeStruct((bsz, g, HEAD_DIM, NSA_REP * t), F32),
        ],
        compiler_params=_params("parallel", "parallel", "parallel"),
        name="cmpattn",
    )(q, kc, vct, ovt)


def _overlap_t(t):
    ncp = t // CMP_STRIDE
    n_cmp = ncp - 1
    n_slc = t // SEL_BLOCK
    starts = np.arange(n_cmp) * CMP_STRIDE
    sel_starts = np.arange(n_slc) * SEL_BLOCK
    ov = ((starts[:, None] < sel_starts[None, :] + SEL_BLOCK)
          & (starts[:, None] + CMP_BLOCK > sel_starts[None, :])).astype(np.float32)
    ovp = np.zeros((ncp, SEL_PAD), np.float32)
    ovp[:n_cmp, :n_slc] = ov
    return jnp.asarray(np.tile(ovp.T, (1, NSA_REP)), dtype=BF16)


def _attn_kernel(qt_ref, ks_ref, vst_ref, kw_ref, vwt_ref, oc_ref, gl_ref, o_ref, *, tq, tk):
    qi = pl.program_id(2)
    t0 = qi * tq
    cols = NSA_REP * tq
    qt = qt_ref[...]
    q_pos = t0 + lax.broadcasted_iota(jnp.int32, (tk, cols), 1) % tq
    k_off = lax.broadcasted_iota(jnp.int32, (tk, cols), 0)

    def flash_step(k_ref, vt_ref, mask_fn):
        def body(kt, carry):
            m, acc = carry
            k0 = pl.multiple_of(kt * tk, tk)
            s = _dot(k_ref[pl.ds(k0, tk), :], qt)
            if mask_fn is not None:
                s = jnp.where(mask_fn(k0 + k_off), s, NEG)
            m_new = jnp.maximum(m, jnp.max(s, axis=0, keepdims=True))
            p = jnp.exp(s - m_new).astype(BF16)
            acc = jnp.exp(m - m_new) * acc + _dot(vt_ref[:, pl.ds(k0, tk)], p)
            return m_new, acc
        return body

    def normalise(carry):
        _, acc = carry
        return acc[0:HEAD_DIM, :] / acc[HEAD_DIM:HEAD_DIM + 1, :]

    init = (jnp.full((1, cols), NEG, F32), jnp.zeros((V_ROWS, cols), F32))

    carry = lax.fori_loop(0, qi, flash_step(ks_ref, vst_ref, None), init)
    o_sel = normalise(flash_step(ks_ref, vst_ref, lambda kp: kp <= q_pos)(qi, carry))

    win_step = flash_step(kw_ref, vwt_ref, lambda kp: (kp <= q_pos) & (q_pos - kp < WINDOW))
    o_win = normalise(lax.fori_loop(jnp.maximum(qi - WINDOW // tk, 0), qi + 1, win_step, init))

    gates = jax.nn.sigmoid(gl_ref[...])
    oc = oc_ref[...]
    for r in range(NSA_REP):
        c = slice(r * tq, (r + 1) * tq)
        o = (gates[r:r + 1, :] * oc[:, c]
             + gates[NSA_REP + r:NSA_REP + r + 1, :] * o_sel[:, c]
             + gates[2 * NSA_REP + r:2 * NSA_REP + r + 1, :] * o_win[:, c])
        o_ref[r * HEAD_DIM:(r + 1) * HEAD_DIM, :] = o.astype(o_ref.dtype)


def _attn(qt, ks, vst, kw, vwt, oc, glt, tq):
    bsz, g, _, t = vst.shape
    cols = NSA_REP * tq
    full_k = pl.BlockSpec((None, None, t, HEAD_DIM + SEL_PAD), lambda b, h, i: (b, h, 0, 0))
    full_v = pl.BlockSpec((None, None, V_ROWS, t), lambda b, h, i: (b, h, 0, 0))
    return pl.pallas_call(
        functools.partial(_attn_kernel, tq=tq, tk=tq),
        grid=(bsz, g, t // tq),
        in_specs=[
            pl.BlockSpec((None, None, HEAD_DIM + SEL_PAD, cols), lambda b, h, i: (b, h, 0, i)),
            full_k, full_v, full_k, full_v,
            pl.BlockSpec((None, None, HEAD_DIM, cols), lambda b, h, i: (b, h, 0, i)),
            pl.BlockSpec((None, None, 4 * NSA_REP, tq), lambda b, h, i: (b, h, 0, i)),
        ],
        out_specs=pl.BlockSpec((None, None, NSA_REP * HEAD_DIM, tq), lambda b, h, i: (b, h, 0, i)),
        out_shape=jax.ShapeDtypeStruct((bsz, g, NSA_REP * HEAD_DIM, t), BF16),
        compiler_params=_params("parallel", "parallel", "arbitrary"),
        name="attn",
    )(qt, ks, vst, kw, vwt, oc, glt)


def _mix_kernel(x_ref, ys_ref, yn_ref, g_ref, wm_ref, wbs_ref, wbn_ref, wo_ref, h_ref):
    x = x_ref[...]
    n1 = _rms(x, g_ref[...]).astype(BF16)
    logits = _dot(n1, wm_ref[...])
    a = _dot(ys_ref[...], wbs_ref[...])
    b = _dot(yn_ref[...], wbn_ref[...])
    mixed = jax.nn.sigmoid(logits[:, :D_MODEL]) * a + jax.nn.sigmoid(logits[:, D_MODEL:]) * b
    h_ref[...] = x + _dot(mixed.astype(BF16), wo_ref[...])


def _mix(x, ys, yn, g, wm, wbs, wbn, wo, tm):
    bsz, t, d = x.shape
    row = lambda w: pl.BlockSpec((None, tm, w), lambda b, i: (b, i, 0))
    return pl.pallas_call(
        _mix_kernel,
        grid=(bsz, t // tm),
        in_specs=[row(d), row(SSM_WIDTH), row(NSA_WIDTH), _const_spec(g.shape), _const_spec(wm.shape),
                  _const_spec(wbs.shape), _const_spec(wbn.shape), _const_spec(wo.shape)],
        out_specs=row(d),
        out_shape=jax.ShapeDtypeStruct((bsz, t, d), F32),
        compiler_params=_params("parallel", "parallel"),
        name="mix",
    )(x, ys, yn, g, wm, wbs, wbn, wo)


def _ffn_kernel(h_ref, p_ref, gf_ref, wa_ref, wb_ref, cwa_ref, cwb_ref, cba_ref, cbb_ref, wd_ref,
                gp_ref, wg_ref, wp_ref, gl_ref, o_ref, acc_ref, ca_ref, cb_ref, *, tm):
    first_tile = pl.program_id(1) == 0
    h = h_ref[...]
    n2 = _rms(h, gf_ref[...]).astype(BF16)
    row = lax.broadcasted_iota(jnp.int32, (tm, FF_CHUNK), 0)

    def conv(hid, carry_ref, c, cw, cb):
        prev = carry_ref[c]
        p1 = jnp.where(row == 0, prev[SUBLANES - 1:SUBLANES, :], pltpu.roll(hid, 1, 0))
        p2 = pltpu.roll(hid, 2, 0)
        p2 = jnp.where(row == 0, prev[SUBLANES - 2:SUBLANES - 1, :], p2)
        p2 = jnp.where(row == 1, prev[SUBLANES - 1:SUBLANES, :], p2)
        carry_ref[c] = hid[tm - SUBLANES:, :]
        return cw[0:1, :] * p2 + cw[1:2, :] * p1 + cw[2:3, :] * hid + cb

    def chunk(c, _):
        a = conv(_dot(n2, wa_ref[c]), ca_ref, c, cwa_ref[c], cba_ref[c])
        b = conv(_dot(n2, wb_ref[c]), cb_ref, c, cwb_ref[c], cbb_ref[c])
        act = (jax.nn.gelu(a) * b).astype(BF16)
        acc_ref[...] += _dot(act, wd_ref[c])
        return 0

    @pl.when(first_tile)
    def _():
        ca_ref[...] = jnp.zeros_like(ca_ref)
        cb_ref[...] = jnp.zeros_like(cb_ref)

    acc_ref[...] = jnp.zeros_like(acc_ref)
    lax.fori_loop(0, N_FF_CHUNKS, chunk, 0)
    h2 = h + acc_ref[...]
    gate = jax.nn.sigmoid(_dot(_rms(h2, gp_ref[...]).astype(BF16), wg_ref[...]))
    h3 = h2 + gate * _dot(p_ref[...].astype(BF16), wp_ref[...])
    o_ref[...] = _rms(h3, gl_ref[...])


def _ffn(h, p, gf, w_up, conv_w, conv_b, w_down, gp, wg, wp, gl, tm):
    bsz, t, d = h.shape
    nc, ch = N_FF_CHUNKS, FF_CHUNK

    def split_cols(m):
        r = m.shape[0]
        m4 = m.reshape(r, 2, nc, ch).transpose(1, 2, 0, 3)
        return m4[0], m4[1]

    wa, wb = split_cols(w_up.astype(BF16))
    cwa, cwb = split_cols(conv_w)
    cba, cbb = split_cols(conv_b.reshape(1, 2 * D_FF))
    wd = w_down.astype(BF16).reshape(nc, ch, d)
    row = lambda w: pl.BlockSpec((None, tm, w), lambda b, i: (b, i, 0))
    consts = [gf, wa, wb, cwa, cwb, cba, cbb, wd, gp, wg, wp, gl]
    return pl.pallas_call(
        functools.partial(_ffn_kernel, tm=tm),
        grid=(bsz, t // tm),
        in_specs=[row(d), row(PLE_DIM)] + [_const_spec(c.shape) for c in consts],
        out_specs=row(d),
        out_shape=jax.ShapeDtypeStruct((bsz, t, d), F32),
        scratch_shapes=[
            pltpu.VMEM((tm, d), F32),
            pltpu.VMEM((nc, SUBLANES, ch), F32),
            pltpu.VMEM((nc, SUBLANES, ch), F32),
        ],
        compiler_params=_params("arbitrary", "arbitrary"),
        name="ffn",
    )(h, p, *consts)


def _layer(x, p, g_mix, w_in, ssm_a_re, ssm_a_im, ssm_log_dt, ssm_b_re, ssm_b_im, ssm_c_re, ssm_c_im,
           ssm_d, ssm_w_glu, cmp_pe_k, cmp_pe_v, cmp_wk1, cmp_wk2, cmp_wv1, cmp_wv2, w_br_ssm, w_br_nsa,
           w_out, g_ffn, w_up, conv_w, conv_b, w_down, g_ple, w_ple_gate, w_ple_proj, g_out):
    bsz, t, d = x.shape
    g, dh = NSA_KV_HEADS, HEAD_DIM
    assert d == D_MODEL and t % WINDOW == 0 and t // SEL_BLOCK <= SEL_PAD
    tq = LANES

    w_main = jnp.concatenate(
        [w_in[:, :OFF_NSA_GATE], jnp.pad(w_in[:, OFF_NSA_GATE:OFF_MERGE], ((0, 0), (0, GATE_PAD - 3 * NSA_HEADS)))],
        axis=1).astype(BF16)
    u_t, q, kv, gate_logits = _inproj(x, g_mix.reshape(1, d), w_main, tm=512)

    wb, wc, ar, ai = _ssm_weights(ssm_a_re, ssm_a_im, ssm_log_dt, ssm_b_re, ssm_b_im, ssm_c_re, ssm_c_im)
    y_ssm = _ssm(u_t.reshape(t * bsz, SSM_WIDTH), wb, wc, ar, ai, ssm_d.reshape(1, SSM_WIDTH),
                 ssm_w_glu.astype(BF16), nb=bsz, tc=64)
    y_ssm = y_ssm.reshape(t, bsz, SSM_WIDTH).transpose(1, 0, 2)

    kvh = kv.reshape(bsz, t, 6, g, dh).transpose(2, 0, 3, 1, 4)
    chunks = lambda z: z.reshape(bsz, g, t // CMP_STRIDE, CMP_STRIDE * dh)
    kc = _compress(chunks(kvh[0]), cmp_pe_k, cmp_wk1, cmp_wk2, transpose_out=False)
    vct = _compress(chunks(kvh[1]), cmp_pe_v, cmp_wv1, cmp_wv2, transpose_out=True)
    qt, o_cmp = _cmpattn(q, kc, vct, _overlap_t(t), tq)

    blk_of_key = np.arange(t)[:, None] // SEL_BLOCK == np.arange(SEL_PAD)[None, :]
    sel_cols = jnp.broadcast_to(jnp.asarray(np.where(blk_of_key, SEL_BIAS, 0.0), dtype=BF16), (bsz, g, t, SEL_PAD))
    ones_rows = jnp.zeros((bsz, g, V_ROWS - dh, t), BF16).at[:, :, 0, :].set(1.0)
    vt = lambda z: jnp.concatenate([z.astype(BF16).transpose(0, 1, 3, 2), ones_rows], axis=2)
    ks = jnp.concatenate([kvh[2].astype(BF16), sel_cols], axis=3)
    kw = jnp.concatenate([kvh[4].astype(BF16), jnp.zeros_like(sel_cols)], axis=3)
    glt = gate_logits[:, :, :3 * NSA_HEADS].reshape(bsz, t, 3, g, NSA_REP).transpose(0, 3, 2, 4, 1)
    glt = jnp.pad(glt.reshape(bsz, g, 3 * NSA_REP, t), ((0, 0), (0, 0), (0, NSA_REP), (0, 0)))
    y_nsa_t = _attn(qt, ks, vt(kvh[3]), kw, vt(kvh[5]), o_cmp, glt, tq)
    y_nsa = y_nsa_t.reshape(bsz, NSA_WIDTH, t).transpose(0, 2, 1)

    h = _mix(x, y_ssm, y_nsa, g_mix.reshape(1, d), w_in[:, OFF_MERGE:].astype(BF16), w_br_ssm.astype(BF16),
             w_br_nsa.astype(BF16), w_out.astype(BF16), tm=512)
    return _ffn(h, p, g_ffn.reshape(1, d), w_up, conv_w, conv_b, w_down, g_ple.reshape(1, d),
                w_ple_gate.astype(BF16), w_ple_proj.astype(BF16), g_out.reshape(1, d), tm=256)


def kernel(x, p, g_mix, w_in, ssm_a_re, ssm_a_im, ssm_log_dt, ssm_b_re, ssm_b_im, ssm_c_re, ssm_c_im, ssm_d, ssm_w_glu, cmp_pe_k, cmp_pe_v, cmp_wk1, cmp_wk2, cmp_wv1, cmp_wv2, w_br_ssm, w_br_nsa, w_out, g_ffn, w_up, conv_w, conv_b, w_down, g_ple, w_ple_gate, w_ple_proj, g_final):
    assert p.shape[0] == 1, "one trunk layer"
    return _layer(x, p[0], g_mix[0], w_in[0], ssm_a_re[0], ssm_a_im[0], ssm_log_dt[0], ssm_b_re[0], ssm_b_im[0],
                  ssm_c_re[0], ssm_c_im[0], ssm_d[0], ssm_w_glu[0], cmp_pe_k[0], cmp_pe_v[0], cmp_wk1[0],
                  cmp_wk2[0], cmp_wv1[0], cmp_wv2[0], w_br_ssm[0], w_br_nsa[0], w_out[0], g_ffn[0], w_up[0],
                  conv_w[0], conv_b[0], w_down[0], g_ple[0], w_ple_gate[0], w_ple_proj[0], g_final)
```

```python
import functools

import numpy as np
import jax
import jax.numpy as jnp
from jax import lax
from jax.experimental import pallas as pl
from jax.experimental.pallas import tpu as pltpu

F32 = jnp.float32
BF16 = jnp.bfloat16

D_MODEL = 1024
SSM_WIDTH = 512
SSM_GROUP = 16
SSM_GROUPS = SSM_WIDTH // SSM_GROUP
SSM_STATE = 64
NSA_HEADS = 8
NSA_KV_HEADS = 2
NSA_REP = NSA_HEADS // NSA_KV_HEADS
HEAD_DIM = 64
NSA_WIDTH = NSA_HEADS * HEAD_DIM
KV_WIDTH = NSA_KV_HEADS * HEAD_DIM
CMP_BLOCK = 32
CMP_STRIDE = 16
CMP_HIDDEN = 128
SEL_BLOCK = 64
N_SELECT = 16
WINDOW = 512
D_FF = 2816
PLE_DIM = 256
EPS = 1e-6
NEG = -1e30
FORCE_BONUS = 100.0

OFF_Q = SSM_WIDTH
OFF_KV = OFF_Q + NSA_WIDTH
OFF_NSA_GATE = OFF_KV + 6 * KV_WIDTH
OFF_MERGE = OFF_NSA_GATE + 3 * NSA_HEADS

LANES = 128
SUBLANES = 8
VMEM_LIMIT = 56 * 1024 * 1024
GATE_PAD = LANES
SEL_PAD = 64
V_ROWS = 80
SEL_BIAS = 2.0 ** 100
SSM_BLOCKS = SSM_WIDTH // LANES
SSM_BLOCK_STATE = (LANES // SSM_GROUP) * SSM_STATE
FF_CHUNK = 256
N_FF_CHUNKS = D_FF // FF_CHUNK


def _rms(x, g):
    return x * lax.rsqrt(jnp.mean(x * x, axis=-1, keepdims=True) + EPS) * g


def _dot(a, b):
    return jnp.dot(a, b, preferred_element_type=F32)


def _params(*sem):
    return pltpu.CompilerParams(dimension_semantics=sem, vmem_limit_bytes=VMEM_LIMIT)


def _const_spec(shape):
    zeros = (0,) * len(shape)
    return pl.BlockSpec(shape, lambda *_: zeros, pipeline_mode=pl.Buffered(1))


def _inproj_kernel(x_ref, g_ref, w_ref, u_ref, q_ref, kv_ref, gl_ref):
    n1 = _rms(x_ref[...], g_ref[...]).astype(BF16)
    z = _dot(n1, w_ref[...])
    u_ref[...] = z[:, :OFF_Q]
    q_ref[...] = z[:, OFF_Q:OFF_KV]
    kv_ref[...] = z[:, OFF_KV:OFF_NSA_GATE]
    gl_ref[...] = z[:, OFF_NSA_GATE:]


def _inproj(x, g, w, tm):
    bsz, t, d = x.shape
    n = w.shape[1]
    return pl.pallas_call(
        _inproj_kernel,
        grid=(bsz, t // tm),
        in_specs=[
            pl.BlockSpec((None, tm, d), lambda b, i: (b, i, 0)),
            _const_spec((1, d)),
            _const_spec((d, n)),
        ],
        out_specs=[
            pl.BlockSpec((tm, SSM_WIDTH), lambda b, i: (i, b)),
            pl.BlockSpec((None, tm, NSA_WIDTH), lambda b, i: (b, i, 0)),
            pl.BlockSpec((None, tm, 6 * KV_WIDTH), lambda b, i: (b, i, 0)),
            pl.BlockSpec((None, tm, GATE_PAD), lambda b, i: (b, i, 0)),
        ],
        out_shape=[
            jax.ShapeDtypeStruct((t, bsz * SSM_WIDTH), F32),
            jax.ShapeDtypeStruct((bsz, t, NSA_WIDTH), F32),
            jax.ShapeDtypeStruct((bsz, t, 6 * KV_WIDTH), F32),
            jax.ShapeDtypeStruct((bsz, t, GATE_PAD), F32),
        ],
        compiler_params=_params("parallel", "parallel"),
        name="inproj",
    )(x, g, w)


def _ssm_kernel(u_ref, wb_ref, wc_ref, ar_ref, ai_ref, d_ref, wglu_ref, o_ref, bu_ref, st_ref, *, tc, nb):
    @pl.when(pl.program_id(0) == 0)
    def _():
        st_ref[...] = jnp.zeros_like(st_ref)

    u = u_ref[...]
    ub = u.astype(BF16)
    w2 = 2 * SSM_BLOCK_STATE
    for cb in range(SSM_BLOCKS):
        bu_ref[:, cb * w2:(cb + 1) * w2] = _dot(ub[:, cb * LANES:(cb + 1) * LANES], wb_ref[cb])

    for cb in range(SSM_BLOCKS):
        re_cols = slice(cb * w2, cb * w2 + SSM_BLOCK_STATE)
        im_cols = slice(cb * w2 + SSM_BLOCK_STATE, (cb + 1) * w2)
        ar = jnp.broadcast_to(ar_ref[cb], (nb, SSM_BLOCK_STATE))
        ai = jnp.broadcast_to(ai_ref[cb], (nb, SSM_BLOCK_STATE))

        def step(t, carry, re_cols=re_cols, im_cols=im_cols, ar=ar, ai=ai):
            sr, si = carry
            rows = pl.ds(pl.multiple_of(t * nb, nb), nb)
            nr = ar * sr - ai * si + bu_ref[rows, re_cols]
            ni = ar * si + ai * sr + bu_ref[rows, im_cols]
            bu_ref[rows, re_cols] = nr
            bu_ref[rows, im_cols] = ni
            return nr, ni

        sr, si = lax.fori_loop(0, tc, step, (st_ref[:, re_cols], st_ref[:, im_cols]), unroll=8)
        st_ref[:, re_cols] = sr
        st_ref[:, im_cols] = si

    ys = [_dot(bu_ref[:, cb * w2:(cb + 1) * w2].astype(BF16), wc_ref[cb]) for cb in range(SSM_BLOCKS)]
    y = jax.nn.gelu(jnp.concatenate(ys, axis=1) + d_ref[...] * u)
    gate = _dot(y.astype(BF16), wglu_ref[...])
    o_ref[...] = (y * jax.nn.sigmoid(gate)).astype(o_ref.dtype)


def _ssm_weights(a_re, a_im, log_dt, b_re, b_im, c_re, c_im):
    gpb = LANES // SSM_GROUP
    dt = jnp.exp(log_dt)[:, None]
    lr = jnp.minimum(a_re, -1e-4)
    li = a_im
    mag = jnp.exp(lr * dt)
    abr = mag * jnp.cos(li * dt)
    abi = mag * jnp.sin(li * dt)
    den = lr * lr + li * li
    qr = ((abr - 1.0) * lr + abi * li) / den
    qi = (abi * lr - (abr - 1.0) * li) / den
    bbr = b_re * qr[:, :, None] - b_im * qi[:, :, None]
    bbi = b_re * qi[:, :, None] + b_im * qr[:, :, None]
    eye = jnp.eye(gpb, dtype=F32)

    def pack_b(m):
        m4 = m.reshape(SSM_BLOCKS, gpb, SSM_STATE, SSM_GROUP)
        return jnp.einsum("kgpc,gh->kgchp", m4, eye).reshape(SSM_BLOCKS, LANES, SSM_BLOCK_STATE)

    def pack_c(m):
        m4 = m.reshape(SSM_BLOCKS, gpb, SSM_GROUP, SSM_STATE)
        return jnp.einsum("kgcp,gh->kgphc", m4, eye).reshape(SSM_BLOCKS, SSM_BLOCK_STATE, LANES)

    wb = jnp.concatenate([pack_b(bbr), pack_b(bbi)], axis=2).astype(BF16)
    wc = jnp.concatenate([pack_c(c_re), -pack_c(c_im)], axis=1).astype(BF16)
    ar = abr.reshape(SSM_BLOCKS, 1, SSM_BLOCK_STATE)
    ai = abi.reshape(SSM_BLOCKS, 1, SSM_BLOCK_STATE)
    return wb, wc, ar, ai


def _ssm(u_tb, wb, wc, ar, ai, d, wglu, nb, tc):
    rows = u_tb.shape[0]
    blk = tc * nb
    w2 = 2 * SSM_BLOCK_STATE
    return pl.pallas_call(
        functools.partial(_ssm_kernel, tc=tc, nb=nb),
        grid=(rows // blk,),
        in_specs=[
            pl.BlockSpec((blk, SSM_WIDTH), lambda i: (i, 0)),
            _const_spec(wb.shape),
            _const_spec(wc.shape),
            _const_spec(ar.shape),
            _const_spec(ai.shape),
            _const_spec(d.shape),
            _const_spec(wglu.shape),
        ],
        out_specs=pl.BlockSpec((blk, SSM_WIDTH), lambda i: (i, 0)),
        out_shape=jax.ShapeDtypeStruct((rows, SSM_WIDTH), BF16),
        scratch_shapes=[
            pltpu.VMEM((blk, SSM_BLOCKS * w2), F32),
            pltpu.VMEM((nb, SSM_BLOCKS * w2), F32),
        ],
        compiler_params=_params("arbitrary"),
        name="ssm",
    )(u_tb, wb, wc, ar, ai, d, wglu)


def _cmp_kernel(x_ref, pe_ref, w1_ref, w2_ref, o_ref, *, transpose_out):
    x = x_ref[...]
    n = x.shape[0]
    first = _dot((x + pe_ref[0]).astype(BF16), w1_ref[0])
    second = _dot((x + pe_ref[1]).astype(BF16), w1_ref[1])
    hid = first + pltpu.roll(second, n - 1, 0)
    hb = jax.nn.gelu(hid).astype(BF16)
    if transpose_out:
        out = lax.dot_general(w2_ref[...], hb, (((1,), (1,)), ((), ())), preferred_element_type=F32)
    else:
        out = _dot(hb, w2_ref[...])
    o_ref[...] = out.astype(o_ref.dtype)


def _compress(chunks, pe, w1, w2, transpose_out):
    bsz, g, n, width = chunks.shape
    pe2 = pe.reshape(2, 1, width)
    w12 = w1.reshape(2, width, CMP_HIDDEN).astype(BF16)
    if transpose_out:
        w2k = w2.T.astype(BF16)
        out_block, out_shape = (None, None, HEAD_DIM, n), (bsz, g, HEAD_DIM, n)
    else:
        w2k = w2.astype(BF16)
        out_block, out_shape = (None, None, n, HEAD_DIM), (bsz, g, n, HEAD_DIM)
    return pl.pallas_call(
        functools.partial(_cmp_kernel, transpose_out=transpose_out),
        grid=(bsz, g),
        in_specs=[
            pl.BlockSpec((None, None, n, width), lambda b, h: (b, h, 0, 0)),
            _const_spec(pe2.shape),
            _const_spec(w12.shape),
            _const_spec(w2k.shape),
        ],
        out_specs=pl.BlockSpec(out_block, lambda b, h: (b, h, 0, 0)),
        out_shape=jax.ShapeDtypeStruct(out_shape, BF16),
        compiler_params=_params("parallel", "parallel"),
        name="compress_v" if transpose_out else "compress_k",
    )(chunks, pe2, w12, w2k)


def _cmpattn_kernel(q_ref, kc_ref, vct_ref, ov_ref, qt_ref, oc_ref, *, tq, n_slc, n_sel):
    t0 = pl.program_id(2) * tq
    qt = (q_ref[...] * (HEAD_DIM ** -0.5)).T.astype(BF16)
    kc = kc_ref[...]
    vct = vct_ref[...]
    ncp = kc.shape[0]
    n_idx = lax.broadcasted_iota(jnp.int32, (ncp, tq), 0)
    t_idx = t0 + lax.broadcasted_iota(jnp.int32, (ncp, tq), 1)
    cmask = n_idx * CMP_STRIDE + (CMP_BLOCK - 1) <= t_idx
    probs = []
    for r in range(NSA_REP):
        qr = qt[r * HEAD_DIM:(r + 1) * HEAD_DIM, :]
        s = jnp.where(cmask, _dot(kc, qr), NEG)
        e = jnp.exp(s - jnp.max(s, axis=0, keepdims=True))
        p = jnp.where(cmask, e / jnp.sum(e, axis=0, keepdims=True), 0.0).astype(BF16)
        probs.append(p)
        oc_ref[:, r * tq:(r + 1) * tq] = _dot(vct, p)
        qt_ref[0:HEAD_DIM, r * tq:(r + 1) * tq] = qr

    imp = _dot(ov_ref[...], jnp.concatenate(probs, axis=0))
    j_idx = lax.broadcasted_iota(jnp.int32, (SEL_PAD, tq), 0)
    cur = (t0 + lax.broadcasted_iota(jnp.int32, (SEL_PAD, tq), 1)) // SEL_BLOCK
    valid = j_idx <= cur
    forced = (j_idx == 0) | (j_idx == cur) | (j_idx == cur - 1)
    score = jnp.where(valid, imp, -1.0) + jnp.where(forced, FORCE_BONUS, 0.0)
    rank = jnp.zeros((SEL_PAD, tq), jnp.int32)
    for j in range(n_slc):
        row = jnp.broadcast_to(score[j:j + 1, :], (SEL_PAD, tq))
        beats = (row > score) | ((row == score) & (j_idx > j))
        rank = rank + beats.astype(jnp.int32)
    sel = (rank < n_sel) & valid
    selm1 = jnp.where(sel, 0.0, -1.0).astype(BF16)
    for r in range(NSA_REP):
        qt_ref[HEAD_DIM:HEAD_DIM + SEL_PAD, r * tq:(r + 1) * tq] = selm1


def _cmpattn(q, kc, vct, ovt, tq):
    bsz, t, _ = q.shape
    g = NSA_KV_HEADS
    ncp = kc.shape[2]
    n_slc = t // SEL_BLOCK
    kern = functools.partial(_cmpattn_kernel, tq=tq, n_slc=n_slc, n_sel=min(N_SELECT, n_slc))
    return pl.pallas_call(
        kern,
        grid=(bsz, g, t // tq),
        in_specs=[
            pl.BlockSpec((None, tq, NSA_REP * HEAD_DIM), lambda b, h, i: (b, i, h)),
            pl.BlockSpec((None, None, ncp, HEAD_DIM), lambda b, h, i: (b, h, 0, 0)),
            pl.BlockSpec((None, None, HEAD_DIM, ncp), lambda b, h, i: (b, h, 0, 0)),
            _const_spec(ovt.shape),
        ],
        out_specs=[
            pl.BlockSpec((None, None, HEAD_DIM + SEL_PAD, NSA_REP * tq), lambda b, h, i: (b, h, 0, i)),
            pl.BlockSpec((None, None, HEAD_DIM, NSA_REP * tq), lambda b, h, i: (b, h, 0, i)),
        ],
        out_shape=[
            jax.ShapeDtypeStruct((bsz, g, HEAD_DIM + SEL_PAD, NSA_REP * t), BF16),
            jax.ShapeDtypeStruct((bsz, g, HEAD_DIM, NSA_REP * t), F32),
        ],
        compiler_params=_params("parallel", "parallel", "parallel"),
        name="cmpattn",
    )(q, kc, vct, ovt)


def _overlap_t(t):
    ncp = t // CMP_STRIDE
    n_cmp = ncp - 1
    n_slc = t // SEL_BLOCK
    starts = np.arange(n_cmp) * CMP_STRIDE
    sel_starts = np.arange(n_slc) * SEL_BLOCK
    ov = ((starts[:, None] < sel_starts[None, :] + SEL_BLOCK)
          & (starts[:, None] + CMP_BLOCK > sel_starts[None, :])).astype(np.float32)
    ovp = np.zeros((ncp, SEL_PAD), np.float32)
    ovp[:n_cmp, :n_slc] = ov
    return jnp.asarray(np.tile(ovp.T, (1, NSA_REP)), dtype=BF16)


def _attn_kernel(qt_ref, ks_ref, vst_ref, kw_ref, vwt_ref, oc_ref, gl_ref, o_ref, *, tq, tk):
    qi = pl.program_id(2)
    t0 = qi * tq
    cols = NSA_REP * tq
    qt = qt_ref[...]
    q_pos = t0 + lax.broadcasted_iota(jnp.int32, (tk, cols), 1) % tq
    k_off = lax.broadcasted_iota(jnp.int32, (tk, cols), 0)

    def flash_step(k_ref, vt_ref, mask_fn):
        def body(kt, carry):
            m, acc = carry
            k0 = pl.multiple_of(kt * tk, tk)
            s = _dot(k_ref[pl.ds(k0, tk), :], qt)
            if mask_fn is not None:
                s = jnp.where(mask_fn(k0 + k_off), s, NEG)
            m_new = jnp.maximum(m, jnp.max(s, axis=0, keepdims=True))
            p = jnp.exp(s - m_new).astype(BF16)
            acc = jnp.exp(m - m_new) * acc + _dot(vt_ref[:, pl.ds(k0, tk)], p)
            return m_new, acc
        return body

    def normalise(carry):
        _, acc = carry
        return acc[0:HEAD_DIM, :] / acc[HEAD_DIM:HEAD_DIM + 1, :]

    init = (jnp.full((1, cols), NEG, F32), jnp.zeros((V_ROWS, cols), F32))
    causal = lambda kp: kp <= q_pos

    sel_step = flash_step(ks_ref, vst_ref, None)

    def sel_pair(j, carry):
        return sel_step(2 * j + 1, sel_step(2 * j, carry))

    carry = lax.fori_loop(0, qi // 2, sel_pair, init)
    carry = lax.fori_loop(2 * (qi // 2), qi, sel_step, carry)
    o_sel = normalise(flash_step(ks_ref, vst_ref, causal)(qi, carry))

    n_back = WINDOW // tk
    oldest = flash_step(kw_ref, vwt_ref, lambda kp: q_pos - kp < WINDOW)
    carry = lax.fori_loop(jnp.maximum(qi - n_back, 0), jnp.maximum(qi - n_back + 1, 0), oldest, init)
    carry = lax.fori_loop(jnp.maximum(qi - n_back + 1, 0), qi, flash_step(kw_ref, vwt_ref, None), carry)
    o_win = normalise(flash_step(kw_ref, vwt_ref, causal)(qi, carry))

    gates = jax.nn.sigmoid(gl_ref[...])
    oc = oc_ref[...]
    for r in range(NSA_REP):
        c = slice(r * tq, (r + 1) * tq)
        o = (gates[r:r + 1, :] * oc[:, c]
             + gates[NSA_REP + r:NSA_REP + r + 1, :] * o_sel[:, c]
             + gates[2 * NSA_REP + r:2 * NSA_REP + r + 1, :] * o_win[:, c])
        o_ref[r * HEAD_DIM:(r + 1) * HEAD_DIM, :] = o.astype(o_ref.dtype)


def _attn(qt, ks, vst, kw, vwt, oc, glt, tq):
    bsz, g, _, t = vst.shape
    cols = NSA_REP * tq
    full_k = pl.BlockSpec((None, None, t, HEAD_DIM + SEL_PAD), lambda b, h, i: (b, h, 0, 0))
    full_v = pl.BlockSpec((None, None, V_ROWS, t), lambda b, h, i: (b, h, 0, 0))
    return pl.pallas_call(
        functools.partial(_attn_kernel, tq=tq, tk=tq),
        grid=(bsz, g, t // tq),
        in_specs=[
            pl.BlockSpec((None, None, HEAD_DIM + SEL_PAD, cols), lambda b, h, i: (b, h, 0, i)),
            full_k, full_v, full_k, full_v,
            pl.BlockSpec((None, None, HEAD_DIM, cols), lambda b, h, i: (b, h, 0, i)),
            pl.BlockSpec((None, None, 4 * NSA_REP, tq), lambda b, h, i: (b, h, 0, i)),
        ],
        out_specs=pl.BlockSpec((None, None, NSA_REP * HEAD_DIM, tq), lambda b, h, i: (b, h, 0, i)),
        out_shape=jax.ShapeDtypeStruct((bsz, g, NSA_REP * HEAD_DIM, t), BF16),
        compiler_params=_params("parallel", "parallel", "arbitrary"),
        name="attn",
    )(qt, ks, vst, kw, vwt, oc, glt)


def _mix_kernel(x_ref, ys_ref, yn_ref, g_ref, wm_ref, wbs_ref, wbn_ref, wo_ref, h_ref):
    x = x_ref[...]
    n1 = _rms(x, g_ref[...]).astype(BF16)
    logits = _dot(n1, wm_ref[...])
    a = _dot(ys_ref[...], wbs_ref[...])
    b = _dot(yn_ref[...], wbn_ref[...])
    mixed = jax.nn.sigmoid(logits[:, :D_MODEL]) * a + jax.nn.sigmoid(logits[:, D_MODEL:]) * b
    h_ref[...] = x + _dot(mixed.astype(BF16), wo_ref[...])


def _mix(x, ys, yn, g, wm, wbs, wbn, wo, tm):
    bsz, t, d = x.shape
    row = lambda w: pl.BlockSpec((None, tm, w), lambda b, i: (b, i, 0))
    return pl.pallas_call(
        _mix_kernel,
        grid=(bsz, t // tm),
        in_specs=[row(d), row(SSM_WIDTH), row(NSA_WIDTH), _const_spec(g.shape), _const_spec(wm.shape),
                  _const_spec(wbs.shape), _const_spec(wbn.shape), _const_spec(wo.shape)],
        out_specs=row(d),
        out_shape=jax.ShapeDtypeStruct((bsz, t, d), F32),
        compiler_params=_params("parallel", "parallel"),
        name="mix",
    )(x, ys, yn, g, wm, wbs, wbn, wo)


def _ffn_kernel(h_ref, p_ref, gf_ref, wa_ref, wb_ref, cwa_ref, cwb_ref, cba_ref, cbb_ref, wd_ref,
                gp_ref, wg_ref, wp_ref, gl_ref, o_ref, ca_ref, cb_ref, *, tm):
    first_tile = pl.program_id(1) == 0
    h = h_ref[...]
    n2 = _rms(h, gf_ref[...]).astype(BF16)
    row = lax.broadcasted_iota(jnp.int32, (tm, FF_CHUNK), 0)

    def conv(hid, carry_ref, c, cw, cb):
        prev = carry_ref[c]
        p1 = jnp.where(row == 0, prev[SUBLANES - 1:SUBLANES, :], pltpu.roll(hid, 1, 0))
        p2 = pltpu.roll(hid, 2, 0)
        p2 = jnp.where(row == 0, prev[SUBLANES - 2:SUBLANES - 1, :], p2)
        p2 = jnp.where(row == 1, prev[SUBLANES - 1:SUBLANES, :], p2)
        carry_ref[c] = hid[tm - SUBLANES:, :]
        return cw[0:1, :] * p2 + cw[1:2, :] * p1 + cw[2:3, :] * hid + cb

    @pl.when(first_tile)
    def _():
        ca_ref[...] = jnp.zeros_like(ca_ref)
        cb_ref[...] = jnp.zeros_like(cb_ref)

    up = lambda c: (_dot(n2, wa_ref[c]), _dot(n2, wb_ref[c]))
    nxt = up(0)
    acc = None
    for c in range(N_FF_CHUNKS):
        ha, hb = nxt
        if c + 1 < N_FF_CHUNKS:
            nxt = up(c + 1)
        a = conv(ha, ca_ref, c, cwa_ref[c], cba_ref[c])
        b = conv(hb, cb_ref, c, cwb_ref[c], cbb_ref[c])
        down = _dot((jax.nn.gelu(a) * b).astype(BF16), wd_ref[c])
        acc = down if acc is None else acc + down
    h2 = h + acc
    gate = jax.nn.sigmoid(_dot(_rms(h2, gp_ref[...]).astype(BF16), wg_ref[...]))
    h3 = h2 + gate * _dot(p_ref[...].astype(BF16), wp_ref[...])
    o_ref[...] = _rms(h3, gl_ref[...])


def _ffn(h, p, gf, w_up, conv_w, conv_b, w_down, gp, wg, wp, gl, tm):
    bsz, t, d = h.shape
    nc, ch = N_FF_CHUNKS, FF_CHUNK

    def split_cols(m):
        r = m.shape[0]
        m4 = m.reshape(r, 2, nc, ch).transpose(1, 2, 0, 3)
        return m4[0], m4[1]

    wa, wb = split_cols(w_up.astype(BF16))
    cwa, cwb = split_cols(conv_w)
    cba, cbb = split_cols(conv_b.reshape(1, 2 * D_FF))
    wd = w_down.astype(BF16).reshape(nc, ch, d)
    row = lambda w: pl.BlockSpec((None, tm, w), lambda b, i: (b, i, 0))
    consts = [gf, wa, wb, cwa, cwb, cba, cbb, wd, gp, wg, wp, gl]
    return pl.pallas_call(
        functools.partial(_ffn_kernel, tm=tm),
        grid=(bsz, t // tm),
        in_specs=[row(d), row(PLE_DIM)] + [_const_spec(c.shape) for c in consts],
        out_specs=row(d),
        out_shape=jax.ShapeDtypeStruct((bsz, t, d), F32),
        scratch_shapes=[
            pltpu.VMEM((nc, SUBLANES, ch), F32),
            pltpu.VMEM((nc, SUBLANES, ch), F32),
        ],
        compiler_params=_params("arbitrary", "arbitrary"),
        name="ffn",
    )(h, p, *consts)


def _layer(x, p, g_mix, w_in, ssm_a_re, ssm_a_im, ssm_log_dt, ssm_b_re, ssm_b_im, ssm_c_re, ssm_c_im,
           ssm_d, ssm_w_glu, cmp_pe_k, cmp_pe_v, cmp_wk1, cmp_wk2, cmp_wv1, cmp_wv2, w_br_ssm, w_br_nsa,
           w_out, g_ffn, w_up, conv_w, conv_b, w_down, g_ple, w_ple_gate, w_ple_proj, g_out):
    bsz, t, d = x.shape
    g, dh = NSA_KV_HEADS, HEAD_DIM
    assert d == D_MODEL and t % WINDOW == 0 and t // SEL_BLOCK <= SEL_PAD
    tq = 2 * LANES

    w_main = jnp.concatenate(
        [w_in[:, :OFF_NSA_GATE], jnp.pad(w_in[:, OFF_NSA_GATE:OFF_MERGE], ((0, 0), (0, GATE_PAD - 3 * NSA_HEADS)))],
        axis=1).astype(BF16)
    u_t, q, kv, gate_logits = _inproj(x, g_mix.reshape(1, d), w_main, tm=512)

    wb, wc, ar, ai = _ssm_weights(ssm_a_re, ssm_a_im, ssm_log_dt, ssm_b_re, ssm_b_im, ssm_c_re, ssm_c_im)
    y_ssm = _ssm(u_t.reshape(t * bsz, SSM_WIDTH), wb, wc, ar, ai, ssm_d.reshape(1, SSM_WIDTH),
                 ssm_w_glu.astype(BF16), nb=bsz, tc=64)
    y_ssm = y_ssm.reshape(t, bsz, SSM_WIDTH).transpose(1, 0, 2)

    kvh = kv.reshape(bsz, t, 6, g, dh).transpose(2, 0, 3, 1, 4)
    chunks = lambda z: z.reshape(bsz, g, t // CMP_STRIDE, CMP_STRIDE * dh)
    kc = _compress(chunks(kvh[0]), cmp_pe_k, cmp_wk1, cmp_wk2, transpose_out=False)
    vct = _compress(chunks(kvh[1]), cmp_pe_v, cmp_wv1, cmp_wv2, transpose_out=True)
    qt, o_cmp = _cmpattn(q, kc, vct, _overlap_t(t), tq)

    blk_of_key = np.arange(t)[:, None] // SEL_BLOCK == np.arange(SEL_PAD)[None, :]
    sel_cols = jnp.broadcast_to(jnp.asarray(np.where(blk_of_key, SEL_BIAS, 0.0), dtype=BF16), (bsz, g, t, SEL_PAD))
    ones_rows = jnp.zeros((bsz, g, V_ROWS - dh, t), BF16).at[:, :, 0, :].set(1.0)
    vt = lambda z: jnp.concatenate([z.astype(BF16).transpose(0, 1, 3, 2), ones_rows], axis=2)
    ks = jnp.concatenate([kvh[2].astype(BF16), sel_cols], axis=3)
    kw = jnp.concatenate([kvh[4].astype(BF16), jnp.zeros_like(sel_cols)], axis=3)
    glt = gate_logits[:, :, :3 * NSA_HEADS].reshape(bsz, t, 3, g, NSA_REP).transpose(0, 3, 2, 4, 1)
    glt = jnp.pad(glt.reshape(bsz, g, 3 * NSA_REP, t), ((0, 0), (0, 0), (0, NSA_REP), (0, 0)))
    y_nsa_t = _attn(qt, ks, vt(kvh[3]), kw, vt(kvh[5]), o_cmp, glt, tq)
    y_nsa = y_nsa_t.reshape(bsz, NSA_WIDTH, t).transpose(0, 2, 1)

    h = _mix(x, y_ssm, y_nsa, g_mix.reshape(1, d), w_in[:, OFF_MERGE:].astype(BF16), w_br_ssm.astype(BF16),
             w_br_nsa.astype(BF16), w_out.astype(BF16), tm=512)
    return _ffn(h, p, g_ffn.reshape(1, d), w_up, conv_w, conv_b, w_down, g_ple.reshape(1, d),
                w_ple_gate.astype(BF16), w_ple_proj.astype(BF16), g_out.reshape(1, d), tm=256)


def kernel(x, p, g_mix, w_in, ssm_a_re, ssm_a_im, ssm_log_dt, ssm_b_re, ssm_b_im, ssm_c_re, ssm_c_im, ssm_d, ssm_w_glu, cmp_pe_k, cmp_pe_v, cmp_wk1, cmp_wk2, cmp_wv1, cmp_wv2, w_br_ssm, w_br_nsa, w_out, g_ffn, w_up, conv_w, conv_b, w_down, g_ple, w_ple_gate, w_ple_proj, g_final):
    assert p.shape[0] == 1, "one trunk layer"
    return _layer(x, p[0], g_mix[0], w_in[0], ssm_a_re[0], ssm_a_im[0], ssm_log_dt[0], ssm_b_re[0], ssm_b_im[0],
                  ssm_c_re[0], ssm_c_im[0], ssm_d[0], ssm_w_glu[0], cmp_pe_k[0], cmp_pe_v[0], cmp_wk1[0],
                  cmp_wk2[0], cmp_wv1[0], cmp_wv2[0], w_br_ssm[0], w_br_nsa[0], w_out[0], g_ffn[0], w_up[0],
                  conv_w[0], conv_b[0], w_down[0], g_ple[0], w_ple_gate[0], w_ple_proj[0], g_final)
```

```python
import functools

import numpy as np
import jax
import jax.numpy as jnp
from jax import lax
from jax.experimental import pallas as pl
from jax.experimental.pallas import tpu as pltpu

F32 = jnp.float32
BF16 = jnp.bfloat16

D_MODEL = 1024
SSM_WIDTH = 512
SSM_GROUP = 16
SSM_GROUPS = SSM_WIDTH // SSM_GROUP
SSM_STATE = 64
NSA_HEADS = 8
NSA_KV_HEADS = 2
NSA_REP = NSA_HEADS // NSA_KV_HEADS
HEAD_DIM = 64
NSA_WIDTH = NSA_HEADS * HEAD_DIM
KV_WIDTH = NSA_KV_HEADS * HEAD_DIM
CMP_BLOCK = 32
CMP_STRIDE = 16
CMP_HIDDEN = 128
SEL_BLOCK = 64
N_SELECT = 16
WINDOW = 512
D_FF = 2816
PLE_DIM = 256
EPS = 1e-6
NEG = -1e30
FORCE_BONUS = 100.0

OFF_Q = SSM_WIDTH
OFF_KV = OFF_Q + NSA_WIDTH
OFF_NSA_GATE = OFF_KV + 6 * KV_WIDTH
OFF_MERGE = OFF_NSA_GATE + 3 * NSA_HEADS

LANES = 128
SUBLANES = 8
BF16_ROWS = 16
VMEM_LIMIT = 56 * 1024 * 1024
GATE_PAD = LANES
SEL_PAD = 64
K_PAD = HEAD_DIM + SEL_PAD
V_ROWS = HEAD_DIM + BF16_ROWS
SEL_BIAS = 2.0 ** 100
LOG2E = 1.4426950408889634
SSM_BLOCKS = SSM_WIDTH // LANES
SSM_BLOCK_STATE = (LANES // SSM_GROUP) * SSM_STATE
SSM_SLABS = 2 * SSM_BLOCK_STATE // LANES
FF_CHUNK = 256
N_FF_CHUNKS = D_FF // FF_CHUNK

TM_INPROJ = 512
TM_MIX = 512
TM_FFN = 256
T_SSM = 64
T_ATTN = 2 * LANES


def _rms(x, g):
    return x * lax.rsqrt(jnp.mean(x * x, axis=-1, keepdims=True) + EPS) * g


def _dot(a, b):
    return jnp.dot(a, b, preferred_element_type=F32)


def _params(*sem):
    return pltpu.CompilerParams(dimension_semantics=sem, vmem_limit_bytes=VMEM_LIMIT)


def _const_spec(shape):
    zeros = (0,) * len(shape)
    return pl.BlockSpec(shape, lambda *_: zeros, pipeline_mode=pl.Buffered(1))


_INPROJ_WIDTHS = (SSM_WIDTH, NSA_WIDTH, 2 * KV_WIDTH, NSA_KV_HEADS * K_PAD, NSA_KV_HEADS * K_PAD,
                  KV_WIDTH, KV_WIDTH, GATE_PAD)
_INPROJ_DTYPES = (F32, F32, F32, BF16, BF16, BF16, BF16, F32)


def _inproj_kernel(x_ref, g_ref, w_ref, *out_refs):
    n1 = _rms(x_ref[...], g_ref[...]).astype(BF16)
    z = _dot(n1, w_ref[...])
    off = 0
    for ref, width in zip(out_refs, _INPROJ_WIDTHS):
        ref[...] = z[:, off:off + width].astype(ref.dtype)
        off += width


def _inproj_weight(w_in):
    d = w_in.shape[0]
    wkv = w_in[:, OFF_KV:OFF_NSA_GATE].reshape(d, 6, NSA_KV_HEADS, HEAD_DIM)
    flat = lambda w: w.reshape(d, KV_WIDTH)
    padk = lambda w: jnp.pad(w, ((0, 0), (0, 0), (0, SEL_PAD))).reshape(d, NSA_KV_HEADS * K_PAD)
    gates = jnp.pad(w_in[:, OFF_NSA_GATE:OFF_MERGE], ((0, 0), (0, GATE_PAD - 3 * NSA_HEADS)))
    return jnp.concatenate([w_in[:, :OFF_KV], flat(wkv[:, 0]), flat(wkv[:, 1]), padk(wkv[:, 2]), padk(wkv[:, 4]),
                            flat(wkv[:, 3]), flat(wkv[:, 5]), gates], axis=1).astype(BF16)


def _inproj(x, g, w):
    bsz, t, d = x.shape
    tm = TM_INPROJ
    row = lambda width: pl.BlockSpec((None, tm, width), lambda b, i: (b, i, 0))
    return pl.pallas_call(
        _inproj_kernel,
        grid=(bsz, t // tm),
        in_specs=[row(d), _const_spec((1, d)), _const_spec(w.shape)],
        out_specs=[row(width) for width in _INPROJ_WIDTHS],
        out_shape=[jax.ShapeDtypeStruct((bsz, t, width), dt) for width, dt in zip(_INPROJ_WIDTHS, _INPROJ_DTYPES)],
        compiler_params=_params("parallel", "parallel"),
        name="inproj",
    )(x, g, w)


def _ssm_kernel(u_ref, wb_ref, wc_ref, ar_ref, ai_ref, d_ref, wglu_ref, o_ref, bu_ref, st_ref, *, tc, nb, pitch):
    @pl.when(pl.program_id(0) == 0)
    def _():
        st_ref[...] = jnp.zeros_like(st_ref)

    u = u_ref[...].reshape(nb * tc, SSM_WIDTH)
    ub = u.astype(BF16)
    half = SSM_SLABS // 2
    for cb in range(SSM_BLOCKS):
        res = _dot(ub[:, cb * LANES:(cb + 1) * LANES], wb_ref[cb])
        for k in range(SSM_SLABS):
            for b in range(nb):
                bu_ref[cb * SSM_SLABS + k, b * pitch:b * pitch + tc, :] = res[b * tc:(b + 1) * tc, k * LANES:(k + 1) * LANES]

    for cb in range(SSM_BLOCKS):
        w2 = 2 * SSM_BLOCK_STATE
        re_cols = slice(cb * w2, cb * w2 + SSM_BLOCK_STATE)
        im_cols = slice(cb * w2 + SSM_BLOCK_STATE, (cb + 1) * w2)
        ar = jnp.broadcast_to(ar_ref[cb], (nb, SSM_BLOCK_STATE))
        ai = jnp.broadcast_to(ai_ref[cb], (nb, SSM_BLOCK_STATE))
        re_slabs = [cb * SSM_SLABS + k for k in range(half)]
        im_slabs = [cb * SSM_SLABS + half + k for k in range(half)]

        def step(t, carry, ar=ar, ai=ai, re_slabs=re_slabs, im_slabs=im_slabs):
            sr, si = carry
            rows = pl.ds(t, nb, stride=pitch)
            br = jnp.concatenate([bu_ref[s, rows, :] for s in re_slabs], axis=1)
            bi = jnp.concatenate([bu_ref[s, rows, :] for s in im_slabs], axis=1)
            nr = ar * sr - ai * si + br
            ni = ar * si + ai * sr + bi
            for k in range(half):
                bu_ref[re_slabs[k], rows, :] = nr[:, k * LANES:(k + 1) * LANES]
                bu_ref[im_slabs[k], rows, :] = ni[:, k * LANES:(k + 1) * LANES]
            return nr, ni

        sr, si = lax.fori_loop(0, tc, step, (st_ref[:, re_cols], st_ref[:, im_cols]), unroll=8)
        st_ref[:, re_cols] = sr
        st_ref[:, im_cols] = si

    ys = []
    for cb in range(SSM_BLOCKS):
        states = jnp.concatenate(
            [jnp.concatenate([bu_ref[cb * SSM_SLABS + k, b * pitch:b * pitch + tc, :] for b in range(nb)], axis=0)
             for k in range(SSM_SLABS)], axis=1)
        ys.append(_dot(states.astype(BF16), wc_ref[cb]))
    y = jax.nn.gelu(jnp.concatenate(ys, axis=1) + d_ref[...] * u)
    gate = _dot(y.astype(BF16), wglu_ref[...])
    o_ref[...] = (y * jax.nn.sigmoid(gate)).astype(o_ref.dtype).reshape(nb, tc, SSM_WIDTH)


def _ssm_weights(a_re, a_im, log_dt, b_re, b_im, c_re, c_im):
    gpb = LANES // SSM_GROUP
    dt = jnp.exp(log_dt)[:, None]
    lr = jnp.minimum(a_re, -1e-4)
    li = a_im
    mag = jnp.exp(lr * dt)
    abr = mag * jnp.cos(li * dt)
    abi = mag * jnp.sin(li * dt)
    den = lr * lr + li * li
    qr = ((abr - 1.0) * lr + abi * li) / den
    qi = (abi * lr - (abr - 1.0) * li) / den
    bbr = b_re * qr[:, :, None] - b_im * qi[:, :, None]
    bbi = b_re * qi[:, :, None] + b_im * qr[:, :, None]
    eye = jnp.eye(gpb, dtype=F32)

    def pack_b(m):
        m4 = m.reshape(SSM_BLOCKS, gpb, SSM_STATE, SSM_GROUP)
        return jnp.einsum("kgpc,gh->kgchp", m4, eye).reshape(SSM_BLOCKS, LANES, SSM_BLOCK_STATE)

    def pack_c(m):
        m4 = m.reshape(SSM_BLOCKS, gpb, SSM_GROUP, SSM_STATE)
        return jnp.einsum("kgcp,gh->kgphc", m4, eye).reshape(SSM_BLOCKS, SSM_BLOCK_STATE, LANES)

    wb = jnp.concatenate([pack_b(bbr), pack_b(bbi)], axis=2).astype(BF16)
    wc = jnp.concatenate([pack_c(c_re), -pack_c(c_im)], axis=1).astype(BF16)
    ar = abr.reshape(SSM_BLOCKS, 1, SSM_BLOCK_STATE)
    ai = abi.reshape(SSM_BLOCKS, 1, SSM_BLOCK_STATE)
    return wb, wc, ar, ai


def _ssm(u, wb, wc, ar, ai, d, wglu):
    bsz, t, _ = u.shape
    tc = T_SSM
    pitch = tc + SUBLANES
    n_slabs = SSM_BLOCKS * SSM_SLABS
    blk = pl.BlockSpec((bsz, tc, SSM_WIDTH), lambda i: (0, i, 0))
    return pl.pallas_call(
        functools.partial(_ssm_kernel, tc=tc, nb=bsz, pitch=pitch),
        grid=(t // tc,),
        in_specs=[blk, _const_spec(wb.shape), _const_spec(wc.shape), _const_spec(ar.shape), _const_spec(ai.shape),
                  _const_spec(d.shape), _const_spec(wglu.shape)],
        out_specs=blk,
        out_shape=jax.ShapeDtypeStruct((bsz, t, SSM_WIDTH), BF16),
        scratch_shapes=[
            pltpu.VMEM((n_slabs, bsz * pitch, LANES), F32),
            pltpu.VMEM((bsz, n_slabs * LANES), F32),
        ],
        compiler_params=_params("arbitrary"),
        name="ssm",
    )(u, wb, wc, ar, ai, d, wglu)


def _cmp_kernel(x_ref, pe_ref, w1_ref, w2_ref, o_ref, *, transpose_out):
    n = x_ref.shape[0] // CMP_STRIDE
    first = second = None
    for j in range(CMP_STRIDE):
        xj = x_ref[pl.ds(j, n, stride=CMP_STRIDE), :]
        a = _dot((xj + pe_ref[j:j + 1, :]).astype(BF16), w1_ref[0, j])
        b = _dot((xj + pe_ref[CMP_STRIDE + j:CMP_STRIDE + j + 1, :]).astype(BF16), w1_ref[1, j])
        first = a if first is None else first + a
        second = b if second is None else second + b
    hid = first + pltpu.roll(second, n - 1, 0)
    hb = jax.nn.gelu(hid).astype(BF16)
    for g in range(NSA_KV_HEADS):
        hg = hb[:, g * CMP_HIDDEN:(g + 1) * CMP_HIDDEN]
        if transpose_out:
            out = lax.dot_general(w2_ref[...], hg, (((1,), (1,)), ((), ())), preferred_element_type=F32)
        else:
            out = _dot(hg, w2_ref[...])
        o_ref[g] = out.astype(o_ref.dtype)


def _compress(kvc, which, pe, w1, w2, transpose_out):
    bsz, t, _ = kvc.shape
    n = t // CMP_STRIDE
    g = NSA_KV_HEADS
    pe2 = jnp.tile(pe, (1, g))
    w1r = w1.reshape(2, CMP_STRIDE, HEAD_DIM, CMP_HIDDEN)
    w1bd = jnp.einsum("hjdc,gk->hjgdkc", w1r, jnp.eye(g, dtype=F32)).reshape(2, CMP_STRIDE, g * HEAD_DIM, g * CMP_HIDDEN)
    w1bd = w1bd.astype(BF16)
    if transpose_out:
        w2k = w2.T.astype(BF16)
        out_block, out_shape = (None, g, HEAD_DIM, n), (bsz, g, HEAD_DIM, n)
    else:
        w2k = w2.astype(BF16)
        out_block, out_shape = (None, g, n, HEAD_DIM), (bsz, g, n, HEAD_DIM)
    return pl.pallas_call(
        functools.partial(_cmp_kernel, transpose_out=transpose_out),
        grid=(bsz,),
        in_specs=[
            pl.BlockSpec((None, t, KV_WIDTH), lambda b: (b, 0, which)),
            _const_spec(pe2.shape),
            _const_spec(w1bd.shape),
            _const_spec(w2k.shape),
        ],
        out_specs=pl.BlockSpec(out_block, lambda b: (b, 0, 0, 0)),
        out_shape=jax.ShapeDtypeStruct(out_shape, BF16),
        compiler_params=_params("parallel"),
        name="compress_v" if transpose_out else "compress_k",
    )(kvc, pe2, w1bd, w2k)


def _cmpattn_kernel(q_ref, kc_ref, vo_ref, qt_ref, oc_ref, *, tq, n_slc, n_sel):
    t0 = pl.program_id(2) * tq
    qt = (q_ref[...] * (HEAD_DIM ** -0.5 * LOG2E)).T.astype(BF16)
    cols = NSA_REP * tq
    qcat = jnp.concatenate([qt[r * HEAD_DIM:(r + 1) * HEAD_DIM, :] for r in range(NSA_REP)], axis=1)
    qt_ref[0:HEAD_DIM, :] = qcat
    kc = kc_ref[...]
    ncp = kc.shape[0]
    n_idx = lax.broadcasted_iota(jnp.int32, (ncp, cols), 0)
    t_idx = t0 + lax.broadcasted_iota(jnp.int32, (ncp, cols), 1) % tq
    cmask = n_idx * CMP_STRIDE + (CMP_BLOCK - 1) <= t_idx
    s = jnp.where(cmask, _dot(kc, qcat), NEG)
    e = jnp.exp2(s - jnp.max(s, axis=0, keepdims=True))
    p = jnp.where(cmask, e / jnp.sum(e, axis=0, keepdims=True), 0.0).astype(BF16)
    both = _dot(vo_ref[...], p)
    oc_ref[...] = both[0:HEAD_DIM, :]
    imp = both[HEAD_DIM:, 0:tq]
    for r in range(1, NSA_REP):
        imp = imp + both[HEAD_DIM:, r * tq:(r + 1) * tq]
    j_idx = lax.broadcasted_iota(jnp.int32, (SEL_PAD, tq), 0)
    cur = (t0 + lax.broadcasted_iota(jnp.int32, (SEL_PAD, tq), 1)) // SEL_BLOCK
    valid = j_idx <= cur
    forced = (j_idx == 0) | (j_idx == cur) | (j_idx == cur - 1)
    score = jnp.where(valid, imp, -1.0) + jnp.where(forced, FORCE_BONUS, 0.0)
    n_grp = SEL_PAD // SUBLANES
    grp = [score[SUBLANES * g:SUBLANES * (g + 1), :] for g in range(n_grp)]
    row_in_grp = lax.broadcasted_iota(jnp.int32, (SUBLANES, tq), 0)
    rank = [jnp.zeros((SUBLANES, tq), F32) for _ in range(n_grp)]
    for j in range(n_slc):
        row = jnp.broadcast_to(score[j:j + 1, :], (SUBLANES, tq))
        for g in range(n_grp):
            if g < j // SUBLANES:
                beats = row > grp[g]
            elif g > j // SUBLANES:
                beats = row >= grp[g]
            else:
                beats = (row > grp[g]) | ((row == grp[g]) & (row_in_grp > j % SUBLANES))
            rank[g] = rank[g] + jnp.where(beats, 1.0, 0.0)
    sel = (jnp.concatenate(rank, axis=0) < n_sel) & valid
    selm1 = jnp.where(sel, 0.0, -1.0).astype(BF16)
    for r in range(NSA_REP):
        qt_ref[HEAD_DIM:K_PAD, r * tq:(r + 1) * tq] = selm1


def _cmpattn(q, kc, vo):
    bsz, t, _ = q.shape
    g = NSA_KV_HEADS
    tq = T_ATTN
    ncp = kc.shape[2]
    n_slc = t // SEL_BLOCK
    kern = functools.partial(_cmpattn_kernel, tq=tq, n_slc=n_slc, n_sel=min(N_SELECT, n_slc))
    return pl.pallas_call(
        kern,
        grid=(bsz, g, t // tq),
        in_specs=[
            pl.BlockSpec((None, tq, NSA_REP * HEAD_DIM), lambda b, h, i: (b, i, h)),
            pl.BlockSpec((None, None, ncp, HEAD_DIM), lambda b, h, i: (b, h, 0, 0)),
            pl.BlockSpec((None, None, HEAD_DIM + SEL_PAD, ncp), lambda b, h, i: (b, h, 0, 0)),
        ],
        out_specs=[
            pl.BlockSpec((None, None, K_PAD, NSA_REP * tq), lambda b, h, i: (b, h, 0, i)),
            pl.BlockSpec((None, None, HEAD_DIM, NSA_REP * tq), lambda b, h, i: (b, h, 0, i)),
        ],
        out_shape=[
            jax.ShapeDtypeStruct((bsz, g, K_PAD, NSA_REP * t), BF16),
            jax.ShapeDtypeStruct((bsz, g, HEAD_DIM, NSA_REP * t), F32),
        ],
        compiler_params=_params("parallel", "parallel", "parallel"),
        name="cmpattn",
    )(q, kc, vo)


def _overlap_t(t):
    ncp = t // CMP_STRIDE
    n_cmp = ncp - 1
    n_slc = t // SEL_BLOCK
    starts = np.arange(n_cmp) * CMP_STRIDE
    sel_starts = np.arange(n_slc) * SEL_BLOCK
    ov = ((starts[:, None] < sel_starts[None, :] + SEL_BLOCK)
          & (starts[:, None] + CMP_BLOCK > sel_starts[None, :])).astype(np.float32)
    ovp = np.zeros((ncp, SEL_PAD), np.float32)
    ovp[:n_cmp, :n_slc] = ov
    return jnp.asarray(ovp.T, dtype=BF16)


def _attn_kernel(qt_ref, ks_ref, kw_ref, vs_ref, vw_ref, e_ref, oc_ref, gl_ref, o_ref, s0_ref, s1_ref, *, tq):
    tk = tq
    qi = pl.program_id(2)
    t0 = qi * tq
    cols = NSA_REP * tq
    qt = qt_ref[...]
    q_pos = t0 + lax.broadcasted_iota(jnp.int32, (tk, cols), 1) % tq
    k_off = lax.broadcasted_iota(jnp.int32, (tk, cols), 0)
    ones_rows = jnp.where(lax.broadcasted_iota(jnp.int32, (BF16_ROWS, tk), 0) == 0, 1.0, 0.0).astype(BF16)

    def tile(kt):
        return pl.ds(pl.multiple_of(kt * tk, tk), tk)

    def sel_scores(kt):
        return _dot(ks_ref[tile(kt), :] + e_ref[tile(kt), :], qt)

    def win_scores(kt):
        return _dot(kw_ref[tile(kt), :], qt)

    def update(carry, scores, v_ref, kt):
        m, acc = carry
        m_new = jnp.maximum(m, jnp.max(scores(), axis=0, keepdims=True))
        p = jnp.exp2(scores() - m_new).astype(BF16)
        vt = jnp.concatenate([v_ref[:, tile(kt)], ones_rows], axis=0)
        return m_new, jnp.exp2(m - m_new) * acc + _dot(vt, p)

    def normalise(carry):
        _, acc = carry
        return acc[0:HEAD_DIM, :] / acc[HEAD_DIM:HEAD_DIM + 1, :]

    def causal(kt, s):
        return jnp.where(kt * tk + k_off <= q_pos, s, NEG)

    init = (jnp.full((1, cols), NEG, F32), jnp.zeros((V_ROWS, cols), F32))

    s0_ref[...] = sel_scores(0)

    def sel_pair(j, carry):
        s1_ref[...] = sel_scores(2 * j + 1)
        carry = update(carry, lambda: s0_ref[...], vs_ref, 2 * j)
        s0_ref[...] = sel_scores(2 * j + 2)
        return update(carry, lambda: s1_ref[...], vs_ref, 2 * j + 1)

    carry = lax.fori_loop(0, qi // 2, sel_pair, init)
    even = 2 * (qi // 2)

    n_back = WINDOW // tk
    assert n_back % 2 == 0

    def win_masked(back):
        kt = jnp.maximum(qi - back, 0)
        s = win_scores(kt)
        if back == 0:
            return kt, causal(qi, s)
        if back == n_back:
            return kt, jnp.where((q_pos - (kt * tk + k_off) < WINDOW) & (qi >= back), s, NEG)
        return kt, jnp.where(qi >= back, s, NEG)

    sel_carry, win_carry = carry, init
    pending = None
    for n, back in enumerate(range(n_back, -1, -1)):
        buf = (s1_ref, s0_ref)[n % 2]
        kt, s = win_masked(back)
        buf[...] = s
        if pending is None:
            sel_carry = update(sel_carry, lambda: causal(even, s0_ref[...]), vs_ref, even)
        else:
            win_carry = update(win_carry, lambda b=pending[0]: b[...], vw_ref, pending[1])
        pending = (buf, kt)
    s0_ref[...] = causal(qi, sel_scores(qi))
    o_win = normalise(update(win_carry, lambda: pending[0][...], vw_ref, pending[1]))
    o_sel = normalise(lax.fori_loop(0, qi % 2, lambda _, c: update(c, lambda: s0_ref[...], vs_ref, qi), sel_carry))

    gates = jax.nn.sigmoid(gl_ref[...])
    oc = oc_ref[...]
    outs = []
    for r in range(NSA_REP):
        c = slice(r * tq, (r + 1) * tq)
        outs.append(gates[r:r + 1, :] * oc[:, c]
                    + gates[NSA_REP + r:NSA_REP + r + 1, :] * o_sel[:, c]
                    + gates[2 * NSA_REP + r:2 * NSA_REP + r + 1, :] * o_win[:, c])
    o_ref[...] = jnp.concatenate(outs, axis=0).T.astype(o_ref.dtype)


def _attn(qt, ks, kw, vst, vwt, e, oc, glt):
    bsz, t, _ = ks.shape
    g = NSA_KV_HEADS
    tq = T_ATTN
    cols = NSA_REP * tq
    full_k = pl.BlockSpec((None, t, K_PAD), lambda b, h, i: (b, 0, h))
    full_v = pl.BlockSpec((None, HEAD_DIM, t), lambda b, h, i: (b, h, 0))
    return pl.pallas_call(
        functools.partial(_attn_kernel, tq=tq),
        grid=(bsz, g, t // tq),
        in_specs=[
            pl.BlockSpec((None, None, K_PAD, cols), lambda b, h, i: (b, h, 0, i)),
            full_k, full_k, full_v, full_v,
            _const_spec(e.shape),
            pl.BlockSpec((None, None, HEAD_DIM, cols), lambda b, h, i: (b, h, 0, i)),
            pl.BlockSpec((None, None, 4 * NSA_REP, tq), lambda b, h, i: (b, h, 0, i)),
        ],
        out_specs=pl.BlockSpec((None, tq, NSA_REP * HEAD_DIM), lambda b, h, i: (b, i, h)),
        out_shape=jax.ShapeDtypeStruct((bsz, t, NSA_WIDTH), BF16),
        scratch_shapes=[pltpu.VMEM((tq, cols), F32)] * 2,
        compiler_params=_params("parallel", "parallel", "arbitrary"),
        name="attn",
    )(qt, ks, kw, vst, vwt, e, oc, glt)


def _mix_kernel(x_ref, ys_ref, yn_ref, g_ref, wm_ref, wbs_ref, wbn_ref, wo_ref, h_ref):
    x = x_ref[...]
    n1 = _rms(x, g_ref[...]).astype(BF16)
    logits = _dot(n1, wm_ref[...])
    a = _dot(ys_ref[...], wbs_ref[...])
    b = _dot(yn_ref[...], wbn_ref[...])
    mixed = jax.nn.sigmoid(logits[:, :D_MODEL]) * a + jax.nn.sigmoid(logits[:, D_MODEL:]) * b
    h_ref[...] = x + _dot(mixed.astype(BF16), wo_ref[...])


def _mix(x, ys, yn, g, wm, wbs, wbn, wo):
    bsz, t, d = x.shape
    tm = TM_MIX
    row = lambda w: pl.BlockSpec((None, tm, w), lambda b, i: (b, i, 0))
    return pl.pallas_call(
        _mix_kernel,
        grid=(bsz, t // tm),
        in_specs=[row(d), row(SSM_WIDTH), row(NSA_WIDTH), _const_spec(g.shape), _const_spec(wm.shape),
                  _const_spec(wbs.shape), _const_spec(wbn.shape), _const_spec(wo.shape)],
        out_specs=row(d),
        out_shape=jax.ShapeDtypeStruct((bsz, t, d), F32),
        compiler_params=_params("parallel", "parallel"),
        name="mix",
    )(x, ys, yn, g, wm, wbs, wbn, wo)


def _ffn_kernel(h_ref, p_ref, gf_ref, wa_ref, wb_ref, cwa_ref, cwb_ref, cba_ref, cbb_ref, wd_ref,
                gp_ref, wg_ref, wp_ref, gl_ref, o_ref, ca_ref, cb_ref, *, tm):
    first_tile = pl.program_id(1) == 0
    h = h_ref[...]
    n2 = _rms(h, gf_ref[...]).astype(BF16)
    def conv(hid, buf_ref, c, cw, cb):
        buf_ref[c, 0:SUBLANES, :] = buf_ref[c, tm:tm + SUBLANES, :]
        buf_ref[c, SUBLANES:SUBLANES + tm, :] = hid
        x1 = buf_ref[c, pl.ds(SUBLANES - 1, tm), :]
        x2 = buf_ref[c, pl.ds(SUBLANES - 2, tm), :]
        return cw[0:1, :] * x2 + cw[1:2, :] * x1 + cw[2:3, :] * hid + cb

    @pl.when(first_tile)
    def _():
        ca_ref[:, tm:tm + SUBLANES, :] = jnp.zeros((N_FF_CHUNKS, SUBLANES, FF_CHUNK), F32)
        cb_ref[:, tm:tm + SUBLANES, :] = jnp.zeros((N_FF_CHUNKS, SUBLANES, FF_CHUNK), F32)

    up = lambda c: (_dot(n2, wa_ref[c]), _dot(n2, wb_ref[c]))
    nxt = up(0)
    acc = None
    for c in range(N_FF_CHUNKS):
        ha, hb = nxt
        if c + 1 < N_FF_CHUNKS:
            nxt = up(c + 1)
        a = conv(ha, ca_ref, c, cwa_ref[c], cba_ref[c])
        b = conv(hb, cb_ref, c, cwb_ref[c], cbb_ref[c])
        down = _dot((jax.nn.gelu(a) * b).astype(BF16), wd_ref[c])
        acc = down if acc is None else acc + down
    h2 = h + acc
    gate = jax.nn.sigmoid(_dot(_rms(h2, gp_ref[...]).astype(BF16), wg_ref[...]))
    h3 = h2 + gate * _dot(p_ref[...].astype(BF16), wp_ref[...])
    o_ref[...] = _rms(h3, gl_ref[...])


def _ffn(h, p, gf, w_up, conv_w, conv_b, w_down, gp, wg, wp, gl):
    bsz, t, d = h.shape
    tm = TM_FFN
    nc, ch = N_FF_CHUNKS, FF_CHUNK

    def split_cols(m):
        r = m.shape[0]
        m4 = m.reshape(r, 2, nc, ch).transpose(1, 2, 0, 3)
        return m4[0], m4[1]

    wa, wb = split_cols(w_up.astype(BF16))
    cwa, cwb = split_cols(conv_w)
    cba, cbb = split_cols(conv_b.reshape(1, 2 * D_FF))
    wd = w_down.astype(BF16).reshape(nc, ch, d)
    row = lambda w: pl.BlockSpec((None, tm, w), lambda b, i: (b, i, 0))
    consts = [gf, wa, wb, cwa, cwb, cba, cbb, wd, gp, wg, wp, gl]
    return pl.pallas_call(
        functools.partial(_ffn_kernel, tm=tm),
        grid=(bsz, t // tm),
        in_specs=[row(d), row(PLE_DIM)] + [_const_spec(c.shape) for c in consts],
        out_specs=row(d),
        out_shape=jax.ShapeDtypeStruct((bsz, t, d), F32),
        scratch_shapes=[
            pltpu.VMEM((nc, SUBLANES + tm, ch), F32),
            pltpu.VMEM((nc, SUBLANES + tm, ch), F32),
        ],
        compiler_params=_params("arbitrary", "arbitrary"),
        name="ffn",
    )(h, p, *consts)


def _layer(x, p, g_mix, w_in, ssm_a_re, ssm_a_im, ssm_log_dt, ssm_b_re, ssm_b_im, ssm_c_re, ssm_c_im,
           ssm_d, ssm_w_glu, cmp_pe_k, cmp_pe_v, cmp_wk1, cmp_wk2, cmp_wv1, cmp_wv2, w_br_ssm, w_br_nsa,
           w_out, g_ffn, w_up, conv_w, conv_b, w_down, g_ple, w_ple_gate, w_ple_proj, g_out):
    bsz, t, d = x.shape
    g = NSA_KV_HEADS
    assert d == D_MODEL and t % WINDOW == 0 and t // SEL_BLOCK <= SEL_PAD

    u, q, kvc, ks, kw, vs, vw, gate_logits = _inproj(x, g_mix.reshape(1, d), _inproj_weight(w_in))

    wb, wc, ar, ai = _ssm_weights(ssm_a_re, ssm_a_im, ssm_log_dt, ssm_b_re, ssm_b_im, ssm_c_re, ssm_c_im)
    y_ssm = _ssm(u, wb, wc, ar, ai, ssm_d.reshape(1, SSM_WIDTH), ssm_w_glu.astype(BF16))

    kc = _compress(kvc, 0, cmp_pe_k, cmp_wk1, cmp_wk2, transpose_out=False)
    vct = _compress(kvc, 1, cmp_pe_v, cmp_wv1, cmp_wv2, transpose_out=True)
    vo = jnp.concatenate([vct, jnp.broadcast_to(_overlap_t(t), (bsz, g, SEL_PAD, t // CMP_STRIDE))], axis=2)
    qt, o_cmp = _cmpattn(q, kc, vo)

    blk_of_key = np.arange(t)[:, None] // SEL_BLOCK == np.arange(SEL_PAD)[None, :]
    sel_cols = np.concatenate([np.zeros((t, HEAD_DIM)), np.where(blk_of_key, SEL_BIAS, 0.0)], axis=1)
    glt = gate_logits[:, :, :3 * NSA_HEADS].reshape(bsz, t, 3, g, NSA_REP).transpose(0, 3, 2, 4, 1)
    glt = jnp.pad(glt.reshape(bsz, g, 3 * NSA_REP, t), ((0, 0), (0, 0), (0, NSA_REP), (0, 0)))
    y_nsa = _attn(qt, ks, kw, vs.transpose(0, 2, 1), vw.transpose(0, 2, 1), jnp.asarray(sel_cols, dtype=BF16),
                  o_cmp, glt)

    h = _mix(x, y_ssm, y_nsa, g_mix.reshape(1, d), w_in[:, OFF_MERGE:].astype(BF16), w_br_ssm.astype(BF16),
             w_br_nsa.astype(BF16), w_out.astype(BF16))
    return _ffn(h, p, g_ffn.reshape(1, d), w_up, conv_w, conv_b, w_down, g_ple.reshape(1, d),
                w_ple_gate.astype(BF16), w_ple_proj.astype(BF16), g_out.reshape(1, d))


def kernel(x, p, g_mix, w_in, ssm_a_re, ssm_a_im, ssm_log_dt, ssm_b_re, ssm_b_im, ssm_c_re, ssm_c_im, ssm_d, ssm_w_glu, cmp_pe_k, cmp_pe_v, cmp_wk1, cmp_wk2, cmp_wv1, cmp_wv2, w_br_ssm, w_br_nsa, w_out, g_ffn, w_up, conv_w, conv_b, w_down, g_ple, w_ple_gate, w_ple_proj, g_final):
    assert p.shape[0] == 1, "one trunk layer"
    return _layer(x, p[0], g_mix[0], w_in[0], ssm_a_re[0], ssm_a_im[0], ssm_log_dt[0], ssm_b_re[0], ssm_b_im[0],
                  ssm_c_re[0], ssm_c_im[0], ssm_d[0], ssm_w_glu[0], cmp_pe_k[0], cmp_pe_v[0], cmp_wk1[0],
                  cmp_wk2[0], cmp_wv1[0], cmp_wv2[0], w_br_ssm[0], w_br_nsa[0], w_out[0], g_ffn[0], w_up[0],
                  conv_w[0], conv_b[0], w_down[0], g_ple[0], w_ple_gate[0], w_ple_proj[0], g_final)
```

```python
import functools

import numpy as np
import jax
import jax.numpy as jnp
from jax import lax
from jax.experimental import pallas as pl
from jax.experimental.pallas import tpu as pltpu

F32 = jnp.float32
BF16 = jnp.bfloat16

D_MODEL = 1024
SSM_WIDTH = 512
SSM_GROUP = 16
SSM_GROUPS = SSM_WIDTH // SSM_GROUP
SSM_STATE = 64
NSA_HEADS = 8
NSA_KV_HEADS = 2
NSA_REP = NSA_HEADS // NSA_KV_HEADS
HEAD_DIM = 64
NSA_WIDTH = NSA_HEADS * HEAD_DIM
KV_WIDTH = NSA_KV_HEADS * HEAD_DIM
CMP_BLOCK = 32
CMP_STRIDE = 16
CMP_HIDDEN = 128
SEL_BLOCK = 64
N_SELECT = 16
WINDOW = 512
D_FF = 2816
PLE_DIM = 256
EPS = 1e-6
NEG = -1e30
FORCE_BONUS = 100.0

OFF_Q = SSM_WIDTH
OFF_KV = OFF_Q + NSA_WIDTH
OFF_NSA_GATE = OFF_KV + 6 * KV_WIDTH
OFF_MERGE = OFF_NSA_GATE + 3 * NSA_HEADS

LANES = 128
SUBLANES = 8
BF16_ROWS = 16
VMEM_LIMIT = 56 * 1024 * 1024
GATE_PAD = LANES
SEL_PAD = 64
K_PAD = HEAD_DIM + SEL_PAD
V_ROWS = HEAD_DIM + BF16_ROWS
SEL_BIAS = 2.0 ** 100
LOG2E = 1.4426950408889634
SSM_BLOCKS = SSM_WIDTH // LANES
SSM_BLOCK_STATE = (LANES // SSM_GROUP) * SSM_STATE
SSM_SLABS = 2 * SSM_BLOCK_STATE // LANES
FF_CHUNK = 256
N_FF_CHUNKS = D_FF // FF_CHUNK
FF_DOWN_GROUP = 4

TM_INPROJ = 512
TM_MIX = 512
TM_FFN = 512
T_SSM = 64
T_ATTN = 2 * LANES


def _rms(x, g):
    return x * lax.rsqrt(jnp.mean(x * x, axis=-1, keepdims=True) + EPS) * g


def _dot(a, b):
    return jnp.dot(a, b, preferred_element_type=F32)


def _params(*sem):
    return pltpu.CompilerParams(dimension_semantics=sem, vmem_limit_bytes=VMEM_LIMIT)


def _const_spec(shape):
    zeros = (0,) * len(shape)
    return pl.BlockSpec(shape, lambda *_: zeros, pipeline_mode=pl.Buffered(1))


_INPROJ_WIDTHS = (SSM_WIDTH, NSA_WIDTH, 2 * KV_WIDTH, NSA_KV_HEADS * K_PAD, NSA_KV_HEADS * K_PAD,
                  KV_WIDTH, KV_WIDTH, GATE_PAD)
_INPROJ_DTYPES = (F32, F32, F32, BF16, BF16, BF16, BF16, F32)


def _inproj_kernel(x_ref, g_ref, w_ref, *out_refs):
    n1 = _rms(x_ref[...], g_ref[...]).astype(BF16)
    z = _dot(n1, w_ref[...])
    off = 0
    for ref, width in zip(out_refs, _INPROJ_WIDTHS):
        ref[...] = z[:, off:off + width].astype(ref.dtype)
        off += width


def _inproj_weight(w_in):
    d = w_in.shape[0]
    wkv = w_in[:, OFF_KV:OFF_NSA_GATE].reshape(d, 6, NSA_KV_HEADS, HEAD_DIM)
    flat = lambda w: w.reshape(d, KV_WIDTH)
    padk = lambda w: jnp.pad(w, ((0, 0), (0, 0), (0, SEL_PAD))).reshape(d, NSA_KV_HEADS * K_PAD)
    gates = jnp.pad(w_in[:, OFF_NSA_GATE:OFF_MERGE], ((0, 0), (0, GATE_PAD - 3 * NSA_HEADS)))
    return jnp.concatenate([w_in[:, :OFF_KV], flat(wkv[:, 0]), flat(wkv[:, 1]), padk(wkv[:, 2]), padk(wkv[:, 4]),
                            flat(wkv[:, 3]), flat(wkv[:, 5]), gates], axis=1).astype(BF16)


def _inproj(x, g, w):
    bsz, t, d = x.shape
    tm = TM_INPROJ
    row = lambda width: pl.BlockSpec((None, tm, width), lambda b, i: (b, i, 0))
    return pl.pallas_call(
        _inproj_kernel,
        grid=(bsz, t // tm),
        in_specs=[row(d), _const_spec((1, d)), _const_spec(w.shape)],
        out_specs=[row(width) for width in _INPROJ_WIDTHS],
        out_shape=[jax.ShapeDtypeStruct((bsz, t, width), dt) for width, dt in zip(_INPROJ_WIDTHS, _INPROJ_DTYPES)],
        compiler_params=_params("parallel", "parallel"),
        name="inproj",
    )(x, g, w)


def _ssm_kernel(u_ref, wb_ref, wc_ref, ar_ref, ai_ref, d_ref, wglu_ref, o_ref, bu_ref, st_ref, *, tc, nb, pitch):
    @pl.when(pl.program_id(0) == 0)
    def _():
        st_ref[...] = jnp.zeros_like(st_ref)

    u = u_ref[...].reshape(nb * tc, SSM_WIDTH)
    ub = u.astype(BF16)
    half = SSM_SLABS // 2
    for cb in range(SSM_BLOCKS):
        res = _dot(ub[:, cb * LANES:(cb + 1) * LANES], wb_ref[cb])
        for k in range(SSM_SLABS):
            for b in range(nb):
                bu_ref[cb * SSM_SLABS + k, b * pitch:b * pitch + tc, :] = res[b * tc:(b + 1) * tc, k * LANES:(k + 1) * LANES]

    for cb in range(SSM_BLOCKS):
        w2 = 2 * SSM_BLOCK_STATE
        re_cols = slice(cb * w2, cb * w2 + SSM_BLOCK_STATE)
        im_cols = slice(cb * w2 + SSM_BLOCK_STATE, (cb + 1) * w2)
        ar = jnp.broadcast_to(ar_ref[cb], (nb, SSM_BLOCK_STATE))
        ai = jnp.broadcast_to(ai_ref[cb], (nb, SSM_BLOCK_STATE))
        re_slabs = [cb * SSM_SLABS + k for k in range(half)]
        im_slabs = [cb * SSM_SLABS + half + k for k in range(half)]

        def step(t, carry, ar=ar, ai=ai, re_slabs=re_slabs, im_slabs=im_slabs):
            sr, si = carry
            rows = pl.ds(t, nb, stride=pitch)
            br = jnp.concatenate([bu_ref[s, rows, :] for s in re_slabs], axis=1)
            bi = jnp.concatenate([bu_ref[s, rows, :] for s in im_slabs], axis=1)
            nr = ar * sr - ai * si + br
            ni = ar * si + ai * sr + bi
            for k in range(half):
                bu_ref[re_slabs[k], rows, :] = nr[:, k * LANES:(k + 1) * LANES]
                bu_ref[im_slabs[k], rows, :] = ni[:, k * LANES:(k + 1) * LANES]
            return nr, ni

        sr, si = lax.fori_loop(0, tc, step, (st_ref[:, re_cols], st_ref[:, im_cols]), unroll=8)
        st_ref[:, re_cols] = sr
        st_ref[:, im_cols] = si

    ys = []
    for cb in range(SSM_BLOCKS):
        states = jnp.concatenate(
            [jnp.concatenate([bu_ref[cb * SSM_SLABS + k, b * pitch:b * pitch + tc, :] for b in range(nb)], axis=0)
             for k in range(SSM_SLABS)], axis=1)
        ys.append(_dot(states.astype(BF16), wc_ref[cb]))
    y = jax.nn.gelu(jnp.concatenate(ys, axis=1) + d_ref[...] * u)
    gate = _dot(y.astype(BF16), wglu_ref[...])
    o_ref[...] = (y * jax.nn.sigmoid(gate)).astype(o_ref.dtype).reshape(nb, tc, SSM_WIDTH)


def _ssm_weights(a_re, a_im, log_dt, b_re, b_im, c_re, c_im):
    gpb = LANES // SSM_GROUP
    dt = jnp.exp(log_dt)[:, None]
    lr = jnp.minimum(a_re, -1e-4)
    li = a_im
    mag = jnp.exp(lr * dt)
    abr = mag * jnp.cos(li * dt)
    abi = mag * jnp.sin(li * dt)
    den = lr * lr + li * li
    qr = ((abr - 1.0) * lr + abi * li) / den
    qi = (abi * lr - (abr - 1.0) * li) / den
    bbr = b_re * qr[:, :, None] - b_im * qi[:, :, None]
    bbi = b_re * qi[:, :, None] + b_im * qr[:, :, None]
    eye = jnp.eye(gpb, dtype=F32)

    def pack_b(m):
        m4 = m.reshape(SSM_BLOCKS, gpb, SSM_STATE, SSM_GROUP)
        return jnp.einsum("kgpc,gh->kgchp", m4, eye).reshape(SSM_BLOCKS, LANES, SSM_BLOCK_STATE)

    def pack_c(m):
        m4 = m.reshape(SSM_BLOCKS, gpb, SSM_GROUP, SSM_STATE)
        return jnp.einsum("kgcp,gh->kgphc", m4, eye).reshape(SSM_BLOCKS, SSM_BLOCK_STATE, LANES)

    wb = jnp.concatenate([pack_b(bbr), pack_b(bbi)], axis=2).astype(BF16)
    wc = jnp.concatenate([pack_c(c_re), -pack_c(c_im)], axis=1).astype(BF16)
    ar = abr.reshape(SSM_BLOCKS, 1, SSM_BLOCK_STATE)
    ai = abi.reshape(SSM_BLOCKS, 1, SSM_BLOCK_STATE)
    return wb, wc, ar, ai


def _ssm(u, wb, wc, ar, ai, d, wglu):
    bsz, t, _ = u.shape
    tc = T_SSM
    pitch = tc + SUBLANES
    n_slabs = SSM_BLOCKS * SSM_SLABS
    blk = pl.BlockSpec((bsz, tc, SSM_WIDTH), lambda i: (0, i, 0))
    return pl.pallas_call(
        functools.partial(_ssm_kernel, tc=tc, nb=bsz, pitch=pitch),
        grid=(t // tc,),
        in_specs=[blk, _const_spec(wb.shape), _const_spec(wc.shape), _const_spec(ar.shape), _const_spec(ai.shape),
                  _const_spec(d.shape), _const_spec(wglu.shape)],
        out_specs=blk,
        out_shape=jax.ShapeDtypeStruct((bsz, t, SSM_WIDTH), BF16),
        scratch_shapes=[
            pltpu.VMEM((n_slabs, bsz * pitch, LANES), F32),
            pltpu.VMEM((bsz, n_slabs * LANES), F32),
        ],
        compiler_params=_params("arbitrary"),
        name="ssm",
    )(u, wb, wc, ar, ai, d, wglu)


def _cmp_kernel(x_ref, pe_ref, w1_ref, w2_ref, o_ref, *, transpose_out):
    n = x_ref.shape[0] // CMP_STRIDE
    first = second = None
    for j in range(CMP_STRIDE):
        xj = x_ref[pl.ds(j, n, stride=CMP_STRIDE), :]
        a = _dot((xj + pe_ref[j:j + 1, :]).astype(BF16), w1_ref[0, j])
        b = _dot((xj + pe_ref[CMP_STRIDE + j:CMP_STRIDE + j + 1, :]).astype(BF16), w1_ref[1, j])
        first = a if first is None else first + a
        second = b if second is None else second + b
    hid = first + pltpu.roll(second, n - 1, 0)
    hb = jax.nn.gelu(hid).astype(BF16)
    for g in range(NSA_KV_HEADS):
        hg = hb[:, g * CMP_HIDDEN:(g + 1) * CMP_HIDDEN]
        if transpose_out:
            out = lax.dot_general(w2_ref[...], hg, (((1,), (1,)), ((), ())), preferred_element_type=F32)
        else:
            out = _dot(hg, w2_ref[...])
        o_ref[g] = out.astype(o_ref.dtype)


def _compress(kvc, which, pe, w1, w2, transpose_out):
    bsz, t, _ = kvc.shape
    n = t // CMP_STRIDE
    g = NSA_KV_HEADS
    pe2 = jnp.tile(pe, (1, g))
    w1r = w1.reshape(2, CMP_STRIDE, HEAD_DIM, CMP_HIDDEN)
    w1bd = jnp.einsum("hjdc,gk->hjgdkc", w1r, jnp.eye(g, dtype=F32)).reshape(2, CMP_STRIDE, g * HEAD_DIM, g * CMP_HIDDEN)
    w1bd = w1bd.astype(BF16)
    if transpose_out:
        w2k = w2.T.astype(BF16)
        out_block, out_shape = (None, g, HEAD_DIM, n), (bsz, g, HEAD_DIM, n)
    else:
        w2k = w2.astype(BF16)
        out_block, out_shape = (None, g, n, HEAD_DIM), (bsz, g, n, HEAD_DIM)
    return pl.pallas_call(
        functools.partial(_cmp_kernel, transpose_out=transpose_out),
        grid=(bsz,),
        in_specs=[
            pl.BlockSpec((None, t, KV_WIDTH), lambda b: (b, 0, which)),
            _const_spec(pe2.shape),
            _const_spec(w1bd.shape),
            _const_spec(w2k.shape),
        ],
        out_specs=pl.BlockSpec(out_block, lambda b: (b, 0, 0, 0)),
        out_shape=jax.ShapeDtypeStruct(out_shape, BF16),
        compiler_params=_params("parallel"),
        name="compress_v" if transpose_out else "compress_k",
    )(kvc, pe2, w1bd, w2k)


def _cmpattn_kernel(q_ref, kc_ref, vo_ref, qt_ref, oc_ref, *, tq, n_slc, n_sel):
    t0 = pl.program_id(2) * tq
    qt = (q_ref[...] * (HEAD_DIM ** -0.5 * LOG2E)).T.astype(BF16)
    cols = NSA_REP * tq
    qcat = jnp.concatenate([qt[r * HEAD_DIM:(r + 1) * HEAD_DIM, :] for r in range(NSA_REP)], axis=1)
    qt_ref[0:HEAD_DIM, :] = qcat
    kc = kc_ref[...]
    ncp = kc.shape[0]
    n_idx = lax.broadcasted_iota(jnp.int32, (ncp, cols), 0)
    t_idx = t0 + lax.broadcasted_iota(jnp.int32, (ncp, cols), 1) % tq
    cmask = n_idx * CMP_STRIDE + (CMP_BLOCK - 1) <= t_idx
    s = jnp.where(cmask, _dot(kc, qcat), NEG)
    e = jnp.exp2(s - jnp.max(s, axis=0, keepdims=True))
    p = jnp.where(cmask, e / jnp.sum(e, axis=0, keepdims=True), 0.0).astype(BF16)
    both = _dot(vo_ref[...], p)
    oc_ref[...] = both[0:HEAD_DIM, :]
    imp = both[HEAD_DIM:, 0:tq]
    for r in range(1, NSA_REP):
        imp = imp + both[HEAD_DIM:, r * tq:(r + 1) * tq]
    j_idx = lax.broadcasted_iota(jnp.int32, (SEL_PAD, tq), 0)
    cur = (t0 + lax.broadcasted_iota(jnp.int32, (SEL_PAD, tq), 1)) // SEL_BLOCK
    valid = j_idx <= cur
    forced = (j_idx == 0) | (j_idx == cur) | (j_idx == cur - 1)
    score = jnp.where(valid, imp, -1.0) + jnp.where(forced, FORCE_BONUS, 0.0)
    n_grp = SEL_PAD // SUBLANES
    grp = [score[SUBLANES * g:SUBLANES * (g + 1), :] for g in range(n_grp)]
    row_in_grp = lax.broadcasted_iota(jnp.int32, (SUBLANES, tq), 0)
    def count_group(rank, jg):
        rank = list(rank)
        for j in range(jg * SUBLANES, min((jg + 1) * SUBLANES, n_slc)):
            row = jnp.broadcast_to(score[j:j + 1, :], (SUBLANES, tq))
            for g in range(n_grp):
                if g < jg:
                    beats = row > grp[g]
                elif g > jg:
                    beats = row >= grp[g]
                else:
                    beats = (row > grp[g]) | ((row == grp[g]) & (row_in_grp > j % SUBLANES))
                rank[g] = rank[g] + jnp.where(beats, 1.0, 0.0)
        return tuple(rank)

    last_grp = (t0 + tq - 1) // SEL_BLOCK // SUBLANES
    rank = count_group(tuple(jnp.zeros((SUBLANES, tq), F32) for _ in range(n_grp)), 0)
    for jg in range(1, pl.cdiv(n_slc, SUBLANES)):
        rank = lax.cond(jg <= last_grp, functools.partial(count_group, jg=jg), lambda r: r, rank)
    sel = (jnp.concatenate(rank, axis=0) < n_sel) & valid
    selm1 = jnp.where(sel, 0.0, -1.0).astype(BF16)
    for r in range(NSA_REP):
        qt_ref[HEAD_DIM:K_PAD, r * tq:(r + 1) * tq] = selm1


def _cmpattn(q, kc, vo):
    bsz, t, _ = q.shape
    g = NSA_KV_HEADS
    tq = T_ATTN
    ncp = kc.shape[2]
    n_slc = t // SEL_BLOCK
    kern = functools.partial(_cmpattn_kernel, tq=tq, n_slc=n_slc, n_sel=min(N_SELECT, n_slc))
    return pl.pallas_call(
        kern,
        grid=(bsz, g, t // tq),
        in_specs=[
            pl.BlockSpec((None, tq, NSA_REP * HEAD_DIM), lambda b, h, i: (b, i, h)),
            pl.BlockSpec((None, None, ncp, HEAD_DIM), lambda b, h, i: (b, h, 0, 0)),
            pl.BlockSpec((None, None, HEAD_DIM + SEL_PAD, ncp), lambda b, h, i: (b, h, 0, 0)),
        ],
        out_specs=[
            pl.BlockSpec((None, None, K_PAD, NSA_REP * tq), lambda b, h, i: (b, h, 0, i)),
            pl.BlockSpec((None, None, HEAD_DIM, NSA_REP * tq), lambda b, h, i: (b, h, 0, i)),
        ],
        out_shape=[
            jax.ShapeDtypeStruct((bsz, g, K_PAD, NSA_REP * t), BF16),
            jax.ShapeDtypeStruct((bsz, g, HEAD_DIM, NSA_REP * t), F32),
        ],
        compiler_params=_params("parallel", "parallel", "parallel"),
        name="cmpattn",
    )(q, kc, vo)


def _overlap_t(t):
    ncp = t // CMP_STRIDE
    n_cmp = ncp - 1
    n_slc = t // SEL_BLOCK
    starts = np.arange(n_cmp) * CMP_STRIDE
    sel_starts = np.arange(n_slc) * SEL_BLOCK
    ov = ((starts[:, None] < sel_starts[None, :] + SEL_BLOCK)
          & (starts[:, None] + CMP_BLOCK > sel_starts[None, :])).astype(np.float32)
    ovp = np.zeros((ncp, SEL_PAD), np.float32)
    ovp[:n_cmp, :n_slc] = ov
    return jnp.asarray(ovp.T, dtype=BF16)


def _attn_kernel(qt_ref, ks_ref, kw_ref, vs_ref, vw_ref, e_ref, oc_ref, gl_ref, o_ref, s0_ref, s1_ref, *, tq):
    tk = tq
    qi = pl.program_id(2)
    t0 = qi * tq
    cols = NSA_REP * tq
    qt = qt_ref[...]
    q_pos = t0 + lax.broadcasted_iota(jnp.int32, (tk, cols), 1) % tq
    k_off = lax.broadcasted_iota(jnp.int32, (tk, cols), 0)
    ones_rows = jnp.where(lax.broadcasted_iota(jnp.int32, (BF16_ROWS, tk), 0) == 0, 1.0, 0.0).astype(BF16)

    def tile(kt):
        return pl.ds(pl.multiple_of(kt * tk, tk), tk)

    def sel_scores(kt):
        return _dot(ks_ref[tile(kt), :] + e_ref[tile(kt), :], qt)

    def win_scores(kt):
        return _dot(kw_ref[tile(kt), :], qt)

    def update(carry, scores, v_ref, kt):
        m, acc = carry
        m_new = jnp.maximum(m, jnp.max(scores(), axis=0, keepdims=True))
        p = jnp.exp2(scores() - m_new).astype(BF16)
        vt = jnp.concatenate([v_ref[:, tile(kt)], ones_rows], axis=0)
        return m_new, jnp.exp2(m - m_new) * acc + _dot(vt, p)

    def normalise(carry):
        _, acc = carry
        return acc[0:HEAD_DIM, :] / acc[HEAD_DIM:HEAD_DIM + 1, :]

    def causal(kt, s):
        return jnp.where(kt * tk + k_off <= q_pos, s, NEG)

    init = (jnp.full((1, cols), NEG, F32), jnp.zeros((V_ROWS, cols), F32))

    s0_ref[...] = sel_scores(0)

    def sel_pair(j, carry):
        s1_ref[...] = sel_scores(2 * j + 1)
        carry = update(carry, lambda: s0_ref[...], vs_ref, 2 * j)
        s0_ref[...] = sel_scores(2 * j + 2)
        return update(carry, lambda: s1_ref[...], vs_ref, 2 * j + 1)

    carry = lax.fori_loop(0, qi // 2, sel_pair, init)
    even = 2 * (qi // 2)

    n_back = WINDOW // tk
    assert n_back % 2 == 0

    def win_masked(back):
        kt = jnp.maximum(qi - back, 0)
        s = win_scores(kt)
        if back == 0:
            return kt, causal(qi, s)
        if back == n_back:
            return kt, jnp.where((q_pos - (kt * tk + k_off) < WINDOW) & (qi >= back), s, NEG)
        return kt, jnp.where(qi >= back, s, NEG)

    sel_carry, win_carry = carry, init
    pending = None
    for n, back in enumerate(range(n_back, -1, -1)):
        buf = (s1_ref, s0_ref)[n % 2]
        kt, s = win_masked(back)
        buf[...] = s
        if pending is None:
            sel_carry = update(sel_carry, lambda: causal(even, s0_ref[...]), vs_ref, even)
        else:
            win_carry = update(win_carry, lambda b=pending[0]: b[...], vw_ref, pending[1])
        pending = (buf, kt)
    s0_ref[...] = causal(qi, sel_scores(qi))
    o_win = normalise(update(win_carry, lambda: pending[0][...], vw_ref, pending[1]))
    o_sel = normalise(lax.fori_loop(0, qi % 2, lambda _, c: update(c, lambda: s0_ref[...], vs_ref, qi), sel_carry))

    gates = jax.nn.sigmoid(gl_ref[...])
    oc = oc_ref[...]
    outs = []
    for r in range(NSA_REP):
        c = slice(r * tq, (r + 1) * tq)
        outs.append(gates[r:r + 1, :] * oc[:, c]
                    + gates[NSA_REP + r:NSA_REP + r + 1, :] * o_sel[:, c]
                    + gates[2 * NSA_REP + r:2 * NSA_REP + r + 1, :] * o_win[:, c])
    o_ref[...] = jnp.concatenate(outs, axis=0).T.astype(o_ref.dtype)


def _attn(qt, ks, kw, vst, vwt, e, oc, glt):
    bsz, t, _ = ks.shape
    g = NSA_KV_HEADS
    tq = T_ATTN
    cols = NSA_REP * tq
    full_k = pl.BlockSpec((None, t, K_PAD), lambda b, h, i: (b, 0, h))
    full_v = pl.BlockSpec((None, HEAD_DIM, t), lambda b, h, i: (b, h, 0))
    return pl.pallas_call(
        functools.partial(_attn_kernel, tq=tq),
        grid=(bsz, g, t // tq),
        in_specs=[
            pl.BlockSpec((None, None, K_PAD, cols), lambda b, h, i: (b, h, 0, i)),
            full_k, full_k, full_v, full_v,
            _const_spec(e.shape),
            pl.BlockSpec((None, None, HEAD_DIM, cols), lambda b, h, i: (b, h, 0, i)),
            pl.BlockSpec((None, None, 4 * NSA_REP, tq), lambda b, h, i: (b, h, 0, i)),
        ],
        out_specs=pl.BlockSpec((None, tq, NSA_REP * HEAD_DIM), lambda b, h, i: (b, i, h)),
        out_shape=jax.ShapeDtypeStruct((bsz, t, NSA_WIDTH), BF16),
        scratch_shapes=[pltpu.VMEM((tq, cols), F32)] * 2,
        compiler_params=_params("parallel", "parallel", "arbitrary"),
        name="attn",
    )(qt, ks, kw, vst, vwt, e, oc, glt)


def _mix_kernel(x_ref, ys_ref, yn_ref, g_ref, wm_ref, wbs_ref, wbn_ref, wo_ref, h_ref):
    x = x_ref[...]
    n1 = _rms(x, g_ref[...]).astype(BF16)
    logits = _dot(n1, wm_ref[...])
    a = _dot(ys_ref[...], wbs_ref[...])
    b = _dot(yn_ref[...], wbn_ref[...])
    mixed = jax.nn.sigmoid(logits[:, :D_MODEL]) * a + jax.nn.sigmoid(logits[:, D_MODEL:]) * b
    h_ref[...] = x + _dot(mixed.astype(BF16), wo_ref[...])


def _mix(x, ys, yn, g, wm, wbs, wbn, wo):
    bsz, t, d = x.shape
    tm = TM_MIX
    row = lambda w: pl.BlockSpec((None, tm, w), lambda b, i: (b, i, 0))
    return pl.pallas_call(
        _mix_kernel,
        grid=(bsz, t // tm),
        in_specs=[row(d), row(SSM_WIDTH), row(NSA_WIDTH), _const_spec(g.shape), _const_spec(wm.shape),
                  _const_spec(wbs.shape), _const_spec(wbn.shape), _const_spec(wo.shape)],
        out_specs=row(d),
        out_shape=jax.ShapeDtypeStruct((bsz, t, d), F32),
        compiler_params=_params("parallel", "parallel"),
        name="mix",
    )(x, ys, yn, g, wm, wbs, wbn, wo)


def _ffn_kernel(h_ref, p_ref, gf_ref, wu_ref, cw_ref, cbias_ref, wd_ref,
                gp_ref, wg_ref, wp_ref, gl_ref, o_ref, ca_ref, cb_ref, act_ref, *, tm):
    first_tile = pl.program_id(1) == 0
    h = h_ref[...]
    n2 = _rms(h, gf_ref[...]).astype(BF16)

    def cols(c, half):
        return slice(half * D_FF + c * FF_CHUNK, half * D_FF + (c + 1) * FF_CHUNK)

    def conv(hid, buf_ref, c, cw, cb):
        buf_ref[c, 0:SUBLANES, :] = buf_ref[c, tm:tm + SUBLANES, :]
        buf_ref[c, SUBLANES:SUBLANES + tm, :] = hid
        x1 = buf_ref[c, pl.ds(SUBLANES - 1, tm), :]
        x2 = buf_ref[c, pl.ds(SUBLANES - 2, tm), :]
        return cw[0:1, :] * x2 + cw[1:2, :] * x1 + cw[2:3, :] * hid + cb

    @pl.when(first_tile)
    def _():
        ca_ref[:, tm:tm + SUBLANES, :] = jnp.zeros((N_FF_CHUNKS, SUBLANES, FF_CHUNK), F32)
        cb_ref[:, tm:tm + SUBLANES, :] = jnp.zeros((N_FF_CHUNKS, SUBLANES, FF_CHUNK), F32)

    up = lambda c: (_dot(n2, wu_ref[:, cols(c, 0)]), _dot(n2, wu_ref[:, cols(c, 1)]))
    nxt = up(0)
    acc = None
    for c in range(N_FF_CHUNKS):
        ha, hb = nxt
        if c + 1 < N_FF_CHUNKS:
            nxt = up(c + 1)
        a = conv(ha, ca_ref, c, cw_ref[:, cols(c, 0)], cbias_ref[:, cols(c, 0)])
        b = conv(hb, cb_ref, c, cw_ref[:, cols(c, 1)], cbias_ref[:, cols(c, 1)])
        act_ref[:, c * FF_CHUNK:(c + 1) * FF_CHUNK] = (jax.nn.gelu(a) * b).astype(BF16)
        if (c + 1) % FF_DOWN_GROUP == 0 or c + 1 == N_FF_CHUNKS:
            lo = (c // FF_DOWN_GROUP) * FF_DOWN_GROUP * FF_CHUNK
            down = _dot(act_ref[:, lo:(c + 1) * FF_CHUNK], wd_ref[lo:(c + 1) * FF_CHUNK, :])
            acc = down if acc is None else acc + down
    h2 = h + acc
    gate = jax.nn.sigmoid(_dot(_rms(h2, gp_ref[...]).astype(BF16), wg_ref[...]))
    h3 = h2 + gate * _dot(p_ref[...].astype(BF16), wp_ref[...])
    o_ref[...] = _rms(h3, gl_ref[...])


def _ffn(h, p, gf, w_up, conv_w, conv_b, w_down, gp, wg, wp, gl):
    bsz, t, d = h.shape
    tm = TM_FFN
    nc, ch = N_FF_CHUNKS, FF_CHUNK

    row = lambda w: pl.BlockSpec((None, tm, w), lambda b, i: (b, i, 0))
    consts = [gf, w_up.astype(BF16), conv_w, conv_b.reshape(1, 2 * D_FF), w_down.astype(BF16), gp, wg, wp, gl]
    return pl.pallas_call(
        functools.partial(_ffn_kernel, tm=tm),
        grid=(bsz, t // tm),
        in_specs=[row(d), row(PLE_DIM)] + [_const_spec(c.shape) for c in consts],
        out_specs=row(d),
        out_shape=jax.ShapeDtypeStruct((bsz, t, d), F32),
        scratch_shapes=[
            pltpu.VMEM((nc, SUBLANES + tm, ch), F32),
            pltpu.VMEM((nc, SUBLANES + tm, ch), F32),
            pltpu.VMEM((tm, D_FF), BF16),
        ],
        compiler_params=_params("arbitrary", "arbitrary"),
        name="ffn",
    )(h, p, *consts)


def _layer(x, p, g_mix, w_in, ssm_a_re, ssm_a_im, ssm_log_dt, ssm_b_re, ssm_b_im, ssm_c_re, ssm_c_im,
           ssm_d, ssm_w_glu, cmp_pe_k, cmp_pe_v, cmp_wk1, cmp_wk2, cmp_wv1, cmp_wv2, w_br_ssm, w_br_nsa,
           w_out, g_ffn, w_up, conv_w, conv_b, w_down, g_ple, w_ple_gate, w_ple_proj, g_out):
    bsz, t, d = x.shape
    g = NSA_KV_HEADS
    assert d == D_MODEL and t % WINDOW == 0 and t // SEL_BLOCK <= SEL_PAD

    u, q, kvc, ks, kw, vs, vw, gate_logits = _inproj(x, g_mix.reshape(1, d), _inproj_weight(w_in))

    wb, wc, ar, ai = _ssm_weights(ssm_a_re, ssm_a_im, ssm_log_dt, ssm_b_re, ssm_b_im, ssm_c_re, ssm_c_im)
    y_ssm = _ssm(u, wb, wc, ar, ai, ssm_d.reshape(1, SSM_WIDTH), ssm_w_glu.astype(BF16))

    kc = _compress(kvc, 0, cmp_pe_k, cmp_wk1, cmp_wk2, transpose_out=False)
    vct = _compress(kvc, 1, cmp_pe_v, cmp_wv1, cmp_wv2, transpose_out=True)
    vo = jnp.concatenate([vct, jnp.broadcast_to(_overlap_t(t), (bsz, g, SEL_PAD, t // CMP_STRIDE))], axis=2)
    qt, o_cmp = _cmpattn(q, kc, vo)

    blk_of_key = np.arange(t)[:, None] // SEL_BLOCK == np.arange(SEL_PAD)[None, :]
    sel_cols = np.concatenate([np.zeros((t, HEAD_DIM)), np.where(blk_of_key, SEL_BIAS, 0.0)], axis=1)
    glt = gate_logits[:, :, :3 * NSA_HEADS].reshape(bsz, t, 3, g, NSA_REP).transpose(0, 3, 2, 4, 1)
    glt = jnp.pad(glt.reshape(bsz, g, 3 * NSA_REP, t), ((0, 0), (0, 0), (0, NSA_REP), (0, 0)))
    y_nsa = _attn(qt, ks, kw, vs.transpose(0, 2, 1), vw.transpose(0, 2, 1), jnp.asarray(sel_cols, dtype=BF16),
                  o_cmp, glt)

    h = _mix(x, y_ssm, y_nsa, g_mix.reshape(1, d), w_in[:, OFF_MERGE:].astype(BF16), w_br_ssm.astype(BF16),
             w_br_nsa.astype(BF16), w_out.astype(BF16))
    return _ffn(h, p, g_ffn.reshape(1, d), w_up, conv_w, conv_b, w_down, g_ple.reshape(1, d),
                w_ple_gate.astype(BF16), w_ple_proj.astype(BF16), g_out.reshape(1, d))


def kernel(x, p, g_mix, w_in, ssm_a_re, ssm_a_im, ssm_log_dt, ssm_b_re, ssm_b_im, ssm_c_re, ssm_c_im, ssm_d, ssm_w_glu, cmp_pe_k, cmp_pe_v, cmp_wk1, cmp_wk2, cmp_wv1, cmp_wv2, w_br_ssm, w_br_nsa, w_out, g_ffn, w_up, conv_w, conv_b, w_down, g_ple, w_ple_gate, w_ple_proj, g_final):
    assert p.shape[0] == 1, "one trunk layer"
    return _layer(x, p[0], g_mix[0], w_in[0], ssm_a_re[0], ssm_a_im[0], ssm_log_dt[0], ssm_b_re[0], ssm_b_im[0],
                  ssm_c_re[0], ssm_c_im[0], ssm_d[0], ssm_w_glu[0], cmp_pe_k[0], cmp_pe_v[0], cmp_wk1[0],
                  cmp_wk2[0], cmp_wv1[0], cmp_wv2[0], w_br_ssm[0], w_br_nsa[0], w_out[0], g_ffn[0], w_up[0],
                  conv_w[0], conv_b[0], w_down[0], g_ple[0], w_ple_gate[0], w_ple_proj[0], g_final)
```

```python
import functools

import numpy as np
import jax
import jax.numpy as jnp
from jax import lax
from jax.experimental import pallas as pl
from jax.experimental.pallas import tpu as pltpu

F32 = jnp.float32
BF16 = jnp.bfloat16

D_MODEL = 1024
SSM_WIDTH = 512
SSM_GROUP = 16
SSM_GROUPS = SSM_WIDTH // SSM_GROUP
SSM_STATE = 64
NSA_HEADS = 8
NSA_KV_HEADS = 2
NSA_REP = NSA_HEADS // NSA_KV_HEADS
HEAD_DIM = 64
NSA_WIDTH = NSA_HEADS * HEAD_DIM
KV_WIDTH = NSA_KV_HEADS * HEAD_DIM
CMP_BLOCK = 32
CMP_STRIDE = 16
CMP_HIDDEN = 128
SEL_BLOCK = 64
N_SELECT = 16
WINDOW = 512
D_FF = 2816
PLE_DIM = 256
EPS = 1e-6
NEG = -1e30
FORCE_BONUS = 100.0

OFF_Q = SSM_WIDTH
OFF_KV = OFF_Q + NSA_WIDTH
OFF_NSA_GATE = OFF_KV + 6 * KV_WIDTH
OFF_MERGE = OFF_NSA_GATE + 3 * NSA_HEADS

LANES = 128
SUBLANES = 8
BF16_ROWS = 16
VMEM_LIMIT = 56 * 1024 * 1024
GATE_PAD = LANES
SEL_PAD = 64
K_PAD = HEAD_DIM + SEL_PAD
V_ROWS = HEAD_DIM + BF16_ROWS
SEL_BIAS = 2.0 ** 100
LOG2E = 1.4426950408889634
SSM_BLOCKS = SSM_WIDTH // LANES
SSM_BLOCK_STATE = (LANES // SSM_GROUP) * SSM_STATE
SSM_SLABS = 2 * SSM_BLOCK_STATE // LANES
FF_CHUNK = 256
N_FF_CHUNKS = D_FF // FF_CHUNK
FF_DOWN_GROUP = 4

TM_INPROJ = 512
TM_MIX = 1024
TM_FFN = 512
T_SSM = 64
T_ATTN = 2 * LANES
ATTN_SUBTILES = 2


def _rms(x, g):
    return x * lax.rsqrt(jnp.mean(x * x, axis=-1, keepdims=True) + EPS) * g


def _dot(a, b):
    return jnp.dot(a, b, preferred_element_type=F32)


def _params(*sem):
    return pltpu.CompilerParams(dimension_semantics=sem, vmem_limit_bytes=VMEM_LIMIT)


def _const_spec(shape):
    zeros = (0,) * len(shape)
    return pl.BlockSpec(shape, lambda *_: zeros, pipeline_mode=pl.Buffered(1))


_INPROJ_WIDTHS = (SSM_WIDTH, NSA_WIDTH, 2 * KV_WIDTH, NSA_KV_HEADS * K_PAD, NSA_KV_HEADS * K_PAD,
                  KV_WIDTH, KV_WIDTH, GATE_PAD)
_INPROJ_DTYPES = (F32, F32, F32, BF16, BF16, BF16, BF16, F32)


def _inproj_kernel(x_ref, g_ref, w_ref, *out_refs):
    n1 = _rms(x_ref[...], g_ref[...]).astype(BF16)
    z = _dot(n1, w_ref[...])
    off = 0
    for ref, width in zip(out_refs, _INPROJ_WIDTHS):
        ref[...] = z[:, off:off + width].astype(ref.dtype)
        off += width


def _inproj_weight(w_in):
    d = w_in.shape[0]
    wkv = w_in[:, OFF_KV:OFF_NSA_GATE].reshape(d, 6, NSA_KV_HEADS, HEAD_DIM)
    flat = lambda w: w.reshape(d, KV_WIDTH)
    padk = lambda w: jnp.pad(w, ((0, 0), (0, 0), (0, SEL_PAD))).reshape(d, NSA_KV_HEADS * K_PAD)
    gates = jnp.pad(w_in[:, OFF_NSA_GATE:OFF_MERGE], ((0, 0), (0, GATE_PAD - 3 * NSA_HEADS)))
    return jnp.concatenate([w_in[:, :OFF_KV], flat(wkv[:, 0]), flat(wkv[:, 1]), padk(wkv[:, 2]), padk(wkv[:, 4]),
                            flat(wkv[:, 3]), flat(wkv[:, 5]), gates], axis=1).astype(BF16)


def _inproj(x, g, w):
    bsz, t, d = x.shape
    tm = TM_INPROJ
    row = lambda width: pl.BlockSpec((None, tm, width), lambda b, i: (b, i, 0))
    return pl.pallas_call(
        _inproj_kernel,
        grid=(bsz, t // tm),
        in_specs=[row(d), _const_spec((1, d)), _const_spec(w.shape)],
        out_specs=[row(width) for width in _INPROJ_WIDTHS],
        out_shape=[jax.ShapeDtypeStruct((bsz, t, width), dt) for width, dt in zip(_INPROJ_WIDTHS, _INPROJ_DTYPES)],
        compiler_params=_params("parallel", "parallel"),
        name="inproj",
    )(x, g, w)


def _ssm_kernel(u_ref, wb_ref, wc_ref, ar_ref, ai_ref, d_ref, wglu_ref, o_ref, bu_ref, xs_ref, yt_ref, st_ref,
                *, tc, nb, pitch):
    @pl.when(pl.program_id(0) == 0)
    def _():
        st_ref[...] = jnp.zeros_like(st_ref)

    u = u_ref[...].reshape(nb * tc, SSM_WIDTH)
    ub = u.astype(BF16)
    half = SSM_SLABS // 2
    for cb in range(SSM_BLOCKS):
        res = _dot(ub[:, cb * LANES:(cb + 1) * LANES], wb_ref[cb])
        for k in range(SSM_SLABS):
            for b in range(nb):
                bu_ref[cb * SSM_SLABS + k, b * pitch:b * pitch + tc, :] = res[b * tc:(b + 1) * tc, k * LANES:(k + 1) * LANES]

    for cb in range(SSM_BLOCKS):
        w2 = 2 * SSM_BLOCK_STATE
        re_cols = slice(cb * w2, cb * w2 + SSM_BLOCK_STATE)
        im_cols = slice(cb * w2 + SSM_BLOCK_STATE, (cb + 1) * w2)
        ar = jnp.broadcast_to(ar_ref[cb], (nb, SSM_BLOCK_STATE))
        ai = jnp.broadcast_to(ai_ref[cb], (nb, SSM_BLOCK_STATE))
        re_slabs = [cb * SSM_SLABS + k for k in range(half)]
        im_slabs = [cb * SSM_SLABS + half + k for k in range(half)]

        def step(t, carry, ar=ar, ai=ai, re_slabs=re_slabs, im_slabs=im_slabs):
            sr, si = carry
            rows = pl.ds(t, nb, stride=pitch)
            br = jnp.concatenate([bu_ref[s, rows, :] for s in re_slabs], axis=1)
            bi = jnp.concatenate([bu_ref[s, rows, :] for s in im_slabs], axis=1)
            nr = ar * sr - ai * si + br
            ni = ar * si + ai * sr + bi
            out_rows = pl.ds(pl.multiple_of(t * nb, nb), nb)
            for k in range(half):
                xs_ref[re_slabs[k], out_rows, :] = nr[:, k * LANES:(k + 1) * LANES]
                xs_ref[im_slabs[k], out_rows, :] = ni[:, k * LANES:(k + 1) * LANES]
            return nr, ni

        sr, si = lax.fori_loop(0, tc, step, (st_ref[:, re_cols], st_ref[:, im_cols]), unroll=8)
        st_ref[:, re_cols] = sr
        st_ref[:, im_cols] = si

    for cb in range(SSM_BLOCKS):
        states = jnp.concatenate([xs_ref[cb * SSM_SLABS + k] for k in range(SSM_SLABS)], axis=1)
        yt_ref[cb] = _dot(states.astype(BF16), wc_ref[cb])
    cx = jnp.concatenate(
        [jnp.concatenate([yt_ref[cb, pl.ds(b, tc, stride=nb), :] for b in range(nb)], axis=0)
         for cb in range(SSM_BLOCKS)], axis=1)
    y = jax.nn.gelu(cx + d_ref[...] * u)
    gate = _dot(y.astype(BF16), wglu_ref[...])
    o_ref[...] = (y * jax.nn.sigmoid(gate)).astype(o_ref.dtype).reshape(nb, tc, SSM_WIDTH)


def _ssm_weights(a_re, a_im, log_dt, b_re, b_im, c_re, c_im):
    gpb = LANES // SSM_GROUP
    dt = jnp.exp(log_dt)[:, None]
    lr = jnp.minimum(a_re, -1e-4)
    li = a_im
    mag = jnp.exp(lr * dt)
    abr = mag * jnp.cos(li * dt)
    abi = mag * jnp.sin(li * dt)
    den = lr * lr + li * li
    qr = ((abr - 1.0) * lr + abi * li) / den
    qi = (abi * lr - (abr - 1.0) * li) / den
    bbr = b_re * qr[:, :, None] - b_im * qi[:, :, None]
    bbi = b_re * qi[:, :, None] + b_im * qr[:, :, None]
    eye = jnp.eye(gpb, dtype=F32)

    def pack_b(m):
        m4 = m.reshape(SSM_BLOCKS, gpb, SSM_STATE, SSM_GROUP)
        return jnp.einsum("kgpc,gh->kgchp", m4, eye).reshape(SSM_BLOCKS, LANES, SSM_BLOCK_STATE)

    def pack_c(m):
        m4 = m.reshape(SSM_BLOCKS, gpb, SSM_GROUP, SSM_STATE)
        return jnp.einsum("kgcp,gh->kgphc", m4, eye).reshape(SSM_BLOCKS, SSM_BLOCK_STATE, LANES)

    wb = jnp.concatenate([pack_b(bbr), pack_b(bbi)], axis=2).astype(BF16)
    wc = jnp.concatenate([pack_c(c_re), -pack_c(c_im)], axis=1).astype(BF16)
    ar = abr.reshape(SSM_BLOCKS, 1, SSM_BLOCK_STATE)
    ai = abi.reshape(SSM_BLOCKS, 1, SSM_BLOCK_STATE)
    return wb, wc, ar, ai


def _ssm(u, wb, wc, ar, ai, d, wglu):
    bsz, t, _ = u.shape
    tc = T_SSM
    pitch = tc + SUBLANES
    n_slabs = SSM_BLOCKS * SSM_SLABS
    blk = pl.BlockSpec((bsz, tc, SSM_WIDTH), lambda i: (0, i, 0))
    return pl.pallas_call(
        functools.partial(_ssm_kernel, tc=tc, nb=bsz, pitch=pitch),
        grid=(t // tc,),
        in_specs=[blk, _const_spec(wb.shape), _const_spec(wc.shape), _const_spec(ar.shape), _const_spec(ai.shape),
                  _const_spec(d.shape), _const_spec(wglu.shape)],
        out_specs=blk,
        out_shape=jax.ShapeDtypeStruct((bsz, t, SSM_WIDTH), BF16),
        scratch_shapes=[
            pltpu.VMEM((n_slabs, bsz * pitch, LANES), F32),
            pltpu.VMEM((n_slabs, tc * bsz, LANES), F32),
            pltpu.VMEM((SSM_BLOCKS, tc * bsz, LANES), F32),
            pltpu.VMEM((bsz, n_slabs * LANES), F32),
        ],
        compiler_params=_params("arbitrary"),
        name="ssm",
    )(u, wb, wc, ar, ai, d, wglu)


def _cmp_kernel(x_ref, pe_ref, w1_ref, w2_ref, o_ref, *, transpose_out):
    n = x_ref.shape[0] // CMP_STRIDE
    first = second = None
    for j in range(CMP_STRIDE):
        xj = x_ref[pl.ds(j, n, stride=CMP_STRIDE), :]
        a = _dot((xj + pe_ref[j:j + 1, :]).astype(BF16), w1_ref[0, j])
        b = _dot((xj + pe_ref[CMP_STRIDE + j:CMP_STRIDE + j + 1, :]).astype(BF16), w1_ref[1, j])
        first = a if first is None else first + a
        second = b if second is None else second + b
    hid = first + pltpu.roll(second, n - 1, 0)
    hb = jax.nn.gelu(hid).astype(BF16)
    for g in range(NSA_KV_HEADS):
        hg = hb[:, g * CMP_HIDDEN:(g + 1) * CMP_HIDDEN]
        if transpose_out:
            out = lax.dot_general(w2_ref[...], hg, (((1,), (1,)), ((), ())), preferred_element_type=F32)
        else:
            out = _dot(hg, w2_ref[...])
        o_ref[g] = out.astype(o_ref.dtype)


def _compress(kvc, which, pe, w1, w2, transpose_out):
    bsz, t, _ = kvc.shape
    n = t // CMP_STRIDE
    g = NSA_KV_HEADS
    pe2 = jnp.tile(pe, (1, g))
    w1r = w1.reshape(2, CMP_STRIDE, HEAD_DIM, CMP_HIDDEN)
    w1bd = jnp.einsum("hjdc,gk->hjgdkc", w1r, jnp.eye(g, dtype=F32)).reshape(2, CMP_STRIDE, g * HEAD_DIM, g * CMP_HIDDEN)
    w1bd = w1bd.astype(BF16)
    if transpose_out:
        w2k = w2.T.astype(BF16)
        out_block, out_shape = (None, g, HEAD_DIM, n), (bsz, g, HEAD_DIM, n)
    else:
        w2k = w2.astype(BF16)
        out_block, out_shape = (None, g, n, HEAD_DIM), (bsz, g, n, HEAD_DIM)
    return pl.pallas_call(
        functools.partial(_cmp_kernel, transpose_out=transpose_out),
        grid=(bsz,),
        in_specs=[
            pl.BlockSpec((None, t, KV_WIDTH), lambda b: (b, 0, which)),
            _const_spec(pe2.shape),
            _const_spec(w1bd.shape),
            _const_spec(w2k.shape),
        ],
        out_specs=pl.BlockSpec(out_block, lambda b: (b, 0, 0, 0)),
        out_shape=jax.ShapeDtypeStruct(out_shape, BF16),
        compiler_params=_params("parallel"),
        name="compress_v" if transpose_out else "compress_k",
    )(kvc, pe2, w1bd, w2k)


def _cmpattn_kernel(q_ref, kc_ref, vo_ref, qt_ref, oc_ref, *, tq, n_sub, n_slc, n_sel):
    cols = NSA_REP * tq
    kc = kc_ref[...]
    ncp = kc.shape[0]
    n_grp = SEL_PAD // SUBLANES
    row_in_grp = lax.broadcasted_iota(jnp.int32, (SUBLANES, tq), 0)

    def scores_of(sub):
        t0 = (pl.program_id(2) * n_sub + sub) * tq
        c0 = sub * cols
        qt = (q_ref[sub * tq:(sub + 1) * tq, :] * (HEAD_DIM ** -0.5 * LOG2E)).T.astype(BF16)
        qcat = jnp.concatenate([qt[r * HEAD_DIM:(r + 1) * HEAD_DIM, :] for r in range(NSA_REP)], axis=1)
        qt_ref[0:HEAD_DIM, c0:c0 + cols] = qcat
        t_idx = t0 + lax.broadcasted_iota(jnp.int32, (1, cols), 1) % tq
        last_visible = jnp.floor_divide(t_idx - (CMP_BLOCK - 1), CMP_STRIDE)
        cmask = lax.broadcasted_iota(jnp.int32, (ncp, cols), 0) <= last_visible
        s = jnp.where(cmask, _dot(kc, qcat), NEG)
        e = jnp.exp2(s - jnp.max(s, axis=0, keepdims=True))
        p = jnp.where(cmask, e * (1.0 / jnp.sum(e, axis=0, keepdims=True)), 0.0).astype(BF16)
        both = _dot(vo_ref[...], p)
        oc_ref[:, c0:c0 + cols] = both[0:HEAD_DIM, :]
        imp = both[HEAD_DIM:, 0:tq]
        for r in range(1, NSA_REP):
            imp = imp + both[HEAD_DIM:, r * tq:(r + 1) * tq]
        j_idx = lax.broadcasted_iota(jnp.int32, (SEL_PAD, tq), 0)
        cur = (t0 + lax.broadcasted_iota(jnp.int32, (SEL_PAD, tq), 1)) // SEL_BLOCK
        valid = j_idx <= cur
        forced = (j_idx == 0) | (j_idx == cur) | (j_idx == cur - 1)
        return jnp.where(valid, imp, -1.0) + jnp.where(forced, FORCE_BONUS, 0.0), valid

    def count_group(rank, score, jg):
        rank = list(rank)
        for j in range(jg * SUBLANES, min((jg + 1) * SUBLANES, n_slc)):
            row = jnp.broadcast_to(score[j:j + 1, :], (SUBLANES, tq))
            for g in range(n_grp):
                grp = score[SUBLANES * g:SUBLANES * (g + 1), :]
                if g < jg:
                    beats = row > grp
                elif g > jg:
                    beats = row >= grp
                else:
                    beats = (row > grp) | ((row == grp) & (row_in_grp > j % SUBLANES))
                rank[g] = rank[g] + jnp.where(beats, 1.0, 0.0)
        return tuple(rank)

    tiles = [scores_of(sub) for sub in range(n_sub)]
    zeros = tuple(jnp.zeros((SUBLANES, tq), F32) for _ in range(n_grp))
    ranks = [count_group(zeros, score, 0) for score, _ in tiles]
    for sub, (score, valid) in enumerate(tiles):
        t0 = (pl.program_id(2) * n_sub + sub) * tq
        last_grp = (t0 + tq - 1) // SEL_BLOCK // SUBLANES
        rank = ranks[sub]
        for jg in range(1, pl.cdiv(n_slc, SUBLANES)):
            rank = lax.cond(jg <= last_grp, functools.partial(count_group, score=score, jg=jg), lambda r: r, rank)
        sel = (jnp.concatenate(rank, axis=0) < n_sel) & valid
        selm1 = jnp.where(sel, 0.0, -1.0).astype(BF16)
        for r in range(NSA_REP):
            qt_ref[HEAD_DIM:K_PAD, sub * cols + r * tq:sub * cols + (r + 1) * tq] = selm1


def _cmpattn(q, kc, vo):
    bsz, t, _ = q.shape
    g = NSA_KV_HEADS
    tq = T_ATTN
    ncp = kc.shape[2]
    n_slc = t // SEL_BLOCK
    n_sub = ATTN_SUBTILES
    kern = functools.partial(_cmpattn_kernel, tq=tq, n_sub=n_sub, n_slc=n_slc, n_sel=min(N_SELECT, n_slc))
    return pl.pallas_call(
        kern,
        grid=(bsz, g, t // (n_sub * tq)),
        in_specs=[
            pl.BlockSpec((None, n_sub * tq, NSA_REP * HEAD_DIM), lambda b, h, i: (b, i, h)),
            pl.BlockSpec((None, None, ncp, HEAD_DIM), lambda b, h, i: (b, h, 0, 0)),
            pl.BlockSpec((None, None, HEAD_DIM + SEL_PAD, ncp), lambda b, h, i: (b, h, 0, 0)),
        ],
        out_specs=[
            pl.BlockSpec((None, None, K_PAD, n_sub * NSA_REP * tq), lambda b, h, i: (b, h, 0, i)),
            pl.BlockSpec((None, None, HEAD_DIM, n_sub * NSA_REP * tq), lambda b, h, i: (b, h, 0, i)),
        ],
        out_shape=[
            jax.ShapeDtypeStruct((bsz, g, K_PAD, NSA_REP * t), BF16),
            jax.ShapeDtypeStruct((bsz, g, HEAD_DIM, NSA_REP * t), F32),
        ],
        compiler_params=_params("parallel", "parallel", "parallel"),
        name="cmpattn",
    )(q, kc, vo)


def _overlap_t(t):
    ncp = t // CMP_STRIDE
    n_cmp = ncp - 1
    n_slc = t // SEL_BLOCK
    starts = np.arange(n_cmp) * CMP_STRIDE
    sel_starts = np.arange(n_slc) * SEL_BLOCK
    ov = ((starts[:, None] < sel_starts[None, :] + SEL_BLOCK)
          & (starts[:, None] + CMP_BLOCK > sel_starts[None, :])).astype(np.float32)
    ovp = np.zeros((ncp, SEL_PAD), np.float32)
    ovp[:n_cmp, :n_slc] = ov
    return jnp.asarray(ovp.T, dtype=BF16)


class _AttnTile:
    def __init__(self, qi, qt, refs, bufs, tq):
        self.qi, self.qt, self.tq, self.tk = qi, qt, tq, tq
        self.ks_ref, self.kw_ref, self.vs_ref, self.vw_ref, self.e_ref = refs
        self.s0_ref, self.s1_ref = bufs
        cols = NSA_REP * tq
        self.q_pos = qi * tq + lax.broadcasted_iota(jnp.int32, (1, cols), 1) % tq
        self.k_off = lax.broadcasted_iota(jnp.int32, (tq, cols), 0)
        self.ones_rows = jnp.where(lax.broadcasted_iota(jnp.int32, (BF16_ROWS, tq), 0) == 0, 1.0, 0.0).astype(BF16)
        self.init = (jnp.full((1, cols), NEG, F32), jnp.zeros((V_ROWS, cols), F32))

    def tile(self, kt):
        return pl.ds(pl.multiple_of(kt * self.tk, self.tk), self.tk)

    def sel_scores(self, kt):
        return _dot(self.ks_ref[self.tile(kt), :] + self.e_ref[self.tile(kt), :], self.qt)

    def win_scores(self, kt):
        return _dot(self.kw_ref[self.tile(kt), :], self.qt)

    def update(self, carry, scores, v_ref, kt):
        m, acc = carry
        m_new = jnp.maximum(m, jnp.max(scores(), axis=0, keepdims=True))
        p = jnp.exp2(scores() - m_new).astype(BF16)
        vt = jnp.concatenate([v_ref[:, self.tile(kt)], self.ones_rows], axis=0)
        return m_new, jnp.exp2(m - m_new) * acc + _dot(vt, p)

    @staticmethod
    def normalise(carry):
        _, acc = carry
        return acc[0:HEAD_DIM, :] * (1.0 / acc[HEAD_DIM:HEAD_DIM + 1, :])

    def causal(self, kt, s):
        return jnp.where(self.k_off <= self.q_pos - kt * self.tk, s, NEG)

    def start(self):
        self.s0_ref[...] = self.sel_scores(0)

    def sel_loop(self):
        def sel_pair(j, carry):
            self.s1_ref[...] = self.sel_scores(2 * j + 1)
            carry = self.update(carry, lambda: self.s0_ref[...], self.vs_ref, 2 * j)
            self.s0_ref[...] = self.sel_scores(2 * j + 2)
            return self.update(carry, lambda: self.s1_ref[...], self.vs_ref, 2 * j + 1)

        self.sel_carry = lax.fori_loop(0, self.qi // 2, sel_pair, self.init)

    def tail(self):
        qi, tk = self.qi, self.tk
        n_back = WINDOW // tk
        assert n_back % 2 == 0
        even = 2 * (qi // 2)

        def win_masked(back):
            kt = jnp.maximum(qi - back, 0)
            s = self.win_scores(kt)
            if back == 0:
                return kt, self.causal(qi, s)
            if back == n_back:
                oldest_visible = jnp.where(qi >= back, self.q_pos - kt * tk - WINDOW, tk)
                return kt, jnp.where(self.k_off > oldest_visible, s, NEG)
            return kt, jnp.where(qi >= back, s, NEG)

        win_carry = self.init
        pending = None
        for n, back in enumerate(range(n_back, -1, -1)):
            buf = (self.s1_ref, self.s0_ref)[n % 2]
            kt, s = win_masked(back)
            buf[...] = s
            if pending is None:
                self.sel_carry = self.update(self.sel_carry, lambda: self.causal(even, self.s0_ref[...]),
                                             self.vs_ref, even)
            else:
                win_carry = self.update(win_carry, lambda b=pending[0]: b[...], self.vw_ref, pending[1])
            pending = (buf, kt)
        self.s0_ref[...] = self.causal(qi, self.sel_scores(qi))
        self.o_win = self.normalise(self.update(win_carry, lambda: pending[0][...], self.vw_ref, pending[1]))

    def odd_tile(self):
        step = lambda _, c: self.update(c, lambda: self.s0_ref[...], self.vs_ref, self.qi)
        self.o_sel = self.normalise(lax.fori_loop(0, self.qi % 2, step, self.sel_carry))


def _attn_kernel(qt_ref, ks_ref, kw_ref, vs_ref, vw_ref, e_ref, oc_ref, gl_ref, o_ref, *bufs, tq, n_sub):
    cols = NSA_REP * tq
    refs = (ks_ref, kw_ref, vs_ref, vw_ref, e_ref)
    tiles = [_AttnTile(pl.program_id(2) * n_sub + sub, qt_ref[:, sub * cols:(sub + 1) * cols], refs,
                       bufs[2 * sub:2 * sub + 2], tq) for sub in range(n_sub)]
    for tl in tiles:
        tl.start()
    for tl in tiles:
        tl.sel_loop()
    for tl in tiles:
        tl.tail()
    for tl in tiles:
        tl.odd_tile()

    for sub, tl in enumerate(tiles):
        gates = jax.nn.sigmoid(gl_ref[:, sub * tq:(sub + 1) * tq])
        oc = oc_ref[:, sub * cols:(sub + 1) * cols]
        outs = []
        for r in range(NSA_REP):
            c = slice(r * tq, (r + 1) * tq)
            outs.append(gates[r:r + 1, :] * oc[:, c]
                        + gates[NSA_REP + r:NSA_REP + r + 1, :] * tl.o_sel[:, c]
                        + gates[2 * NSA_REP + r:2 * NSA_REP + r + 1, :] * tl.o_win[:, c])
        o_ref[sub * tq:(sub + 1) * tq, :] = jnp.concatenate(outs, axis=0).T.astype(o_ref.dtype)


def _attn(qt, ks, kw, vst, vwt, e, oc, glt):
    bsz, t, _ = ks.shape
    g = NSA_KV_HEADS
    tq = T_ATTN
    cols = NSA_REP * tq
    full_k = pl.BlockSpec((None, t, K_PAD), lambda b, h, i: (b, 0, h))
    full_v = pl.BlockSpec((None, HEAD_DIM, t), lambda b, h, i: (b, h, 0))
    n_sub = ATTN_SUBTILES
    return pl.pallas_call(
        functools.partial(_attn_kernel, tq=tq, n_sub=n_sub),
        grid=(bsz, g, t // (n_sub * tq)),
        in_specs=[
            pl.BlockSpec((None, None, K_PAD, n_sub * cols), lambda b, h, i: (b, h, 0, i)),
            full_k, full_k, full_v, full_v,
            _const_spec(e.shape),
            pl.BlockSpec((None, None, HEAD_DIM, n_sub * cols), lambda b, h, i: (b, h, 0, i)),
            pl.BlockSpec((None, None, 4 * NSA_REP, n_sub * tq), lambda b, h, i: (b, h, 0, i)),
        ],
        out_specs=pl.BlockSpec((None, n_sub * tq, NSA_REP * HEAD_DIM), lambda b, h, i: (b, i, h)),
        out_shape=jax.ShapeDtypeStruct((bsz, t, NSA_WIDTH), BF16),
        scratch_shapes=[pltpu.VMEM((tq, cols), F32)] * (2 * n_sub),
        compiler_params=_params("parallel", "parallel", "arbitrary"),
        name="attn",
    )(qt, ks, kw, vst, vwt, e, oc, glt)


def _mix_kernel(x_ref, ys_ref, yn_ref, g_ref, wm_ref, wbs_ref, wbn_ref, wo_ref, h_ref):
    x = x_ref[...]
    n1 = _rms(x, g_ref[...]).astype(BF16)
    logits = _dot(n1, wm_ref[...])
    a = _dot(ys_ref[...], wbs_ref[...])
    b = _dot(yn_ref[...], wbn_ref[...])
    mixed = jax.nn.sigmoid(logits[:, :D_MODEL]) * a + jax.nn.sigmoid(logits[:, D_MODEL:]) * b
    h_ref[...] = x + _dot(mixed.astype(BF16), wo_ref[...])


def _mix(x, ys, yn, g, wm, wbs, wbn, wo):
    bsz, t, d = x.shape
    tm = TM_MIX
    row = lambda w: pl.BlockSpec((None, tm, w), lambda b, i: (b, i, 0))
    return pl.pallas_call(
        _mix_kernel,
        grid=(bsz, t // tm),
        in_specs=[row(d), row(SSM_WIDTH), row(NSA_WIDTH), _const_spec(g.shape), _const_spec(wm.shape),
                  _const_spec(wbs.shape), _const_spec(wbn.shape), _const_spec(wo.shape)],
        out_specs=row(d),
        out_shape=jax.ShapeDtypeStruct((bsz, t, d), F32),
        compiler_params=_params("parallel", "parallel"),
        name="mix",
    )(x, ys, yn, g, wm, wbs, wbn, wo)


def _ffn_kernel(h_ref, p_ref, gf_ref, wu_ref, cw_ref, cbias_ref, wd_ref,
                gp_ref, wg_ref, wp_ref, gl_ref, o_ref, ca_ref, cb_ref, act_ref, *, tm):
    first_tile = pl.program_id(1) == 0
    h = h_ref[...]
    n2 = _rms(h, gf_ref[...]).astype(BF16)

    def cols(c, half):
        return slice(half * D_FF + c * FF_CHUNK, half * D_FF + (c + 1) * FF_CHUNK)

    def conv(hid, buf_ref, c, cw, cb):
        buf_ref[c, 0:SUBLANES, :] = buf_ref[c, tm:tm + SUBLANES, :]
        buf_ref[c, SUBLANES:SUBLANES + tm, :] = hid
        x1 = buf_ref[c, pl.ds(SUBLANES - 1, tm), :]
        x2 = buf_ref[c, pl.ds(SUBLANES - 2, tm), :]
        return cw[0:1, :] * x2 + cw[1:2, :] * x1 + cw[2:3, :] * hid + cb

    @pl.when(first_tile)
    def _():
        ca_ref[:, tm:tm + SUBLANES, :] = jnp.zeros((N_FF_CHUNKS, SUBLANES, FF_CHUNK), F32)
        cb_ref[:, tm:tm + SUBLANES, :] = jnp.zeros((N_FF_CHUNKS, SUBLANES, FF_CHUNK), F32)

    up = lambda c: (_dot(n2, wu_ref[:, cols(c, 0)]), _dot(n2, wu_ref[:, cols(c, 1)]))
    nxt = up(0)
    acc = None
    for c in range(N_FF_CHUNKS):
        ha, hb = nxt
        if c + 1 < N_FF_CHUNKS:
            nxt = up(c + 1)
        a = conv(ha, ca_ref, c, cw_ref[:, cols(c, 0)], cbias_ref[:, cols(c, 0)])
        b = conv(hb, cb_ref, c, cw_ref[:, cols(c, 1)], cbias_ref[:, cols(c, 1)])
        act_ref[:, c * FF_CHUNK:(c + 1) * FF_CHUNK] = (jax.nn.gelu(a) * b).astype(BF16)
        if (c + 1) % FF_DOWN_GROUP == 0 or c + 1 == N_FF_CHUNKS:
            lo = (c // FF_DOWN_GROUP) * FF_DOWN_GROUP * FF_CHUNK
            down = _dot(act_ref[:, lo:(c + 1) * FF_CHUNK], wd_ref[lo:(c + 1) * FF_CHUNK, :])
            acc = down if acc is None else acc + down
    h2 = h + acc
    gate = jax.nn.sigmoid(_dot(_rms(h2, gp_ref[...]).astype(BF16), wg_ref[...]))
    h3 = h2 + gate * _dot(p_ref[...].astype(BF16), wp_ref[...])
    o_ref[...] = _rms(h3, gl_ref[...])


def _ffn(h, p, gf, w_up, conv_w, conv_b, w_down, gp, wg, wp, gl):
    bsz, t, d = h.shape
    tm = TM_FFN
    nc, ch = N_FF_CHUNKS, FF_CHUNK

    row = lambda w: pl.BlockSpec((None, tm, w), lambda b, i: (b, i, 0))
    consts = [gf, w_up.astype(BF16), conv_w, conv_b.reshape(1, 2 * D_FF), w_down.astype(BF16), gp, wg, wp, gl]
    return pl.pallas_call(
        functools.partial(_ffn_kernel, tm=tm),
        grid=(bsz, t // tm),
        in_specs=[row(d), row(PLE_DIM)] + [_const_spec(c.shape) for c in consts],
        out_specs=row(d),
        out_shape=jax.ShapeDtypeStruct((bsz, t, d), F32),
        scratch_shapes=[
            pltpu.VMEM((nc, SUBLANES + tm, ch), F32),
            pltpu.VMEM((nc, SUBLANES + tm, ch), F32),
            pltpu.VMEM((tm, D_FF), BF16),
        ],
        compiler_params=_params("arbitrary", "arbitrary"),
        name="ffn",
    )(h, p, *consts)


def _layer(x, p, g_mix, w_in, ssm_a_re, ssm_a_im, ssm_log_dt, ssm_b_re, ssm_b_im, ssm_c_re, ssm_c_im,
           ssm_d, ssm_w_glu, cmp_pe_k, cmp_pe_v, cmp_wk1, cmp_wk2, cmp_wv1, cmp_wv2, w_br_ssm, w_br_nsa,
           w_out, g_ffn, w_up, conv_w, conv_b, w_down, g_ple, w_ple_gate, w_ple_proj, g_out):
    bsz, t, d = x.shape
    g = NSA_KV_HEADS
    assert d == D_MODEL and t % WINDOW == 0 and t // SEL_BLOCK <= SEL_PAD

    u, q, kvc, ks, kw, vs, vw, gate_logits = _inproj(x, g_mix.reshape(1, d), _inproj_weight(w_in))

    wb, wc, ar, ai = _ssm_weights(ssm_a_re, ssm_a_im, ssm_log_dt, ssm_b_re, ssm_b_im, ssm_c_re, ssm_c_im)
    y_ssm = _ssm(u, wb, wc, ar, ai, ssm_d.reshape(1, SSM_WIDTH), ssm_w_glu.astype(BF16))

    kc = _compress(kvc, 0, cmp_pe_k, cmp_wk1, cmp_wk2, transpose_out=False)
    vct = _compress(kvc, 1, cmp_pe_v, cmp_wv1, cmp_wv2, transpose_out=True)
    vo = jnp.concatenate([vct, jnp.broadcast_to(_overlap_t(t), (bsz, g, SEL_PAD, t // CMP_STRIDE))], axis=2)
    qt, o_cmp = _cmpattn(q, kc, vo)

    blk_of_key = np.arange(t)[:, None] // SEL_BLOCK == np.arange(SEL_PAD)[None, :]
    sel_cols = np.concatenate([np.zeros((t, HEAD_DIM)), np.where(blk_of_key, SEL_BIAS, 0.0)], axis=1)
    glt = gate_logits[:, :, :3 * NSA_HEADS].reshape(bsz, t, 3, g, NSA_REP).transpose(0, 3, 2, 4, 1)
    glt = jnp.pad(glt.reshape(bsz, g, 3 * NSA_REP, t), ((0, 0), (0, 0), (0, NSA_REP), (0, 0)))
    y_nsa = _attn(qt, ks, kw, vs.transpose(0, 2, 1), vw.transpose(0, 2, 1), jnp.asarray(sel_cols, dtype=BF16),
                  o_cmp, glt)

    h = _mix(x, y_ssm, y_nsa, g_mix.reshape(1, d), w_in[:, OFF_MERGE:].astype(BF16), w_br_ssm.astype(BF16),
             w_br_nsa.astype(BF16), w_out.astype(BF16))
    return _ffn(h, p, g_ffn.reshape(1, d), w_up, conv_w, conv_b, w_down, g_ple.reshape(1, d),
                w_ple_gate.astype(BF16), w_ple_proj.astype(BF16), g_out.reshape(1, d))


def kernel(x, p, g_mix, w_in, ssm_a_re, ssm_a_im, ssm_log_dt, ssm_b_re, ssm_b_im, ssm_c_re, ssm_c_im, ssm_d, ssm_w_glu, cmp_pe_k, cmp_pe_v, cmp_wk1, cmp_wk2, cmp_wv1, cmp_wv2, w_br_ssm, w_br_nsa, w_out, g_ffn, w_up, conv_w, conv_b, w_down, g_ple, w_ple_gate, w_ple_proj, g_final):
    assert p.shape[0] == 1, "one trunk layer"
    return _layer(x, p[0], g_mix[0], w_in[0], ssm_a_re[0], ssm_a_im[0], ssm_log_dt[0], ssm_b_re[0], ssm_b_im[0],
                  ssm_c_re[0], ssm_c_im[0], ssm_d[0], ssm_w_glu[0], cmp_pe_k[0], cmp_pe_v[0], cmp_wk1[0],
                  cmp_wk2[0], cmp_wv1[0], cmp_wv2[0], w_br_ssm[0], w_br_nsa[0], w_out[0], g_ffn[0], w_up[0],
                  conv_w[0], conv_b[0], w_down[0], g_ple[0], w_ple_gate[0], w_ple_proj[0], g_final)
```

```python
import functools

import numpy as np
import jax
import jax.numpy as jnp
from jax import lax
from jax.experimental import pallas as pl
from jax.experimental.pallas import tpu as pltpu

F32 = jnp.float32
BF16 = jnp.bfloat16

D_MODEL = 1024
SSM_WIDTH = 512
SSM_GROUP = 16
SSM_GROUPS = SSM_WIDTH // SSM_GROUP
SSM_STATE = 64
NSA_HEADS = 8
NSA_KV_HEADS = 2
NSA_REP = NSA_HEADS // NSA_KV_HEADS
HEAD_DIM = 64
NSA_WIDTH = NSA_HEADS * HEAD_DIM
KV_WIDTH = NSA_KV_HEADS * HEAD_DIM
CMP_BLOCK = 32
CMP_STRIDE = 16
CMP_HIDDEN = 128
SEL_BLOCK = 64
N_SELECT = 16
WINDOW = 512
D_FF = 2816
PLE_DIM = 256
EPS = 1e-6
NEG = -1e30
FORCE_BONUS = 100.0

OFF_Q = SSM_WIDTH
OFF_KV = OFF_Q + NSA_WIDTH
OFF_NSA_GATE = OFF_KV + 6 * KV_WIDTH
OFF_MERGE = OFF_NSA_GATE + 3 * NSA_HEADS

LANES = 128
SUBLANES = 8
BF16_ROWS = 16
VMEM_LIMIT = 56 * 1024 * 1024
GATE_PAD = LANES
SEL_PAD = 64
K_PAD = HEAD_DIM + SEL_PAD
V_ROWS = HEAD_DIM + BF16_ROWS
SEL_BIAS = 2.0 ** 100
LOG2E = 1.4426950408889634
SSM_BLOCKS = SSM_WIDTH // LANES
SSM_BLOCK_STATE = (LANES // SSM_GROUP) * SSM_STATE
SSM_SLABS = 2 * SSM_BLOCK_STATE // LANES
FF_CHUNK = 256
N_FF_CHUNKS = D_FF // FF_CHUNK
FF_DOWN_GROUP = 4

TM_INPROJ = 512
TM_MIX = 1024
TM_FFN = 512
T_SSM = 64
T_ATTN = 2 * LANES
ATTN_SUBTILES = 4


def _rms(x, g):
    return x * lax.rsqrt(jnp.mean(x * x, axis=-1, keepdims=True) + EPS) * g


def _dot(a, b):
    return jnp.dot(a, b, preferred_element_type=F32)


def _params(*sem):
    return pltpu.CompilerParams(dimension_semantics=sem, vmem_limit_bytes=VMEM_LIMIT)


def _const_spec(shape):
    zeros = (0,) * len(shape)
    return pl.BlockSpec(shape, lambda *_: zeros, pipeline_mode=pl.Buffered(1))


_INPROJ_WIDTHS = (SSM_WIDTH, NSA_WIDTH, 2 * KV_WIDTH, NSA_KV_HEADS * K_PAD, NSA_KV_HEADS * K_PAD,
                  KV_WIDTH, KV_WIDTH, GATE_PAD)
_INPROJ_DTYPES = (F32, F32, F32, BF16, BF16, BF16, BF16, F32)


def _inproj_kernel(x_ref, g_ref, w_ref, *out_refs):
    n1 = _rms(x_ref[...], g_ref[...]).astype(BF16)
    z = _dot(n1, w_ref[...])
    off = 0
    for ref, width in zip(out_refs, _INPROJ_WIDTHS):
        ref[...] = z[:, off:off + width].astype(ref.dtype)
        off += width


def _inproj_weight(w_in):
    d = w_in.shape[0]
    wkv = w_in[:, OFF_KV:OFF_NSA_GATE].reshape(d, 6, NSA_KV_HEADS, HEAD_DIM)
    flat = lambda w: w.reshape(d, KV_WIDTH)
    padk = lambda w: jnp.pad(w, ((0, 0), (0, 0), (0, SEL_PAD))).reshape(d, NSA_KV_HEADS * K_PAD)
    gates = jnp.pad(w_in[:, OFF_NSA_GATE:OFF_MERGE], ((0, 0), (0, GATE_PAD - 3 * NSA_HEADS)))
    return jnp.concatenate([w_in[:, :OFF_KV], flat(wkv[:, 0]), flat(wkv[:, 1]), padk(wkv[:, 2]), padk(wkv[:, 4]),
                            flat(wkv[:, 3]), flat(wkv[:, 5]), gates], axis=1).astype(BF16)


def _inproj(x, g, w):
    bsz, t, d = x.shape
    tm = TM_INPROJ
    row = lambda width: pl.BlockSpec((None, tm, width), lambda b, i: (b, i, 0))
    return pl.pallas_call(
        _inproj_kernel,
        grid=(bsz, t // tm),
        in_specs=[row(d), _const_spec((1, d)), _const_spec(w.shape)],
        out_specs=[row(width) for width in _INPROJ_WIDTHS],
        out_shape=[jax.ShapeDtypeStruct((bsz, t, width), dt) for width, dt in zip(_INPROJ_WIDTHS, _INPROJ_DTYPES)],
        compiler_params=_params("parallel", "parallel"),
        name="inproj",
    )(x, g, w)


def _ssm_kernel(u_ref, wb_ref, wc_ref, ar_ref, ai_ref, d_ref, wglu_ref, o_ref, bu_ref, xs_ref, yt_ref, st_ref,
                *, tc, nb, pitch):
    @pl.when(pl.program_id(0) == 0)
    def _():
        st_ref[...] = jnp.zeros_like(st_ref)

    u = u_ref[...].reshape(nb * tc, SSM_WIDTH)
    ub = u.astype(BF16)
    half = SSM_SLABS // 2
    for cb in range(SSM_BLOCKS):
        res = _dot(ub[:, cb * LANES:(cb + 1) * LANES], wb_ref[cb])
        for k in range(SSM_SLABS):
            for b in range(nb):
                bu_ref[cb * SSM_SLABS + k, b * pitch:b * pitch + tc, :] = res[b * tc:(b + 1) * tc, k * LANES:(k + 1) * LANES]

    for cb in range(SSM_BLOCKS):
        w2 = 2 * SSM_BLOCK_STATE
        re_cols = slice(cb * w2, cb * w2 + SSM_BLOCK_STATE)
        im_cols = slice(cb * w2 + SSM_BLOCK_STATE, (cb + 1) * w2)
        ar = jnp.broadcast_to(ar_ref[cb], (nb, SSM_BLOCK_STATE))
        ai = jnp.broadcast_to(ai_ref[cb], (nb, SSM_BLOCK_STATE))
        re_slabs = [cb * SSM_SLABS + k for k in range(half)]
        im_slabs = [cb * SSM_SLABS + half + k for k in range(half)]

        def step(t, carry, ar=ar, ai=ai, re_slabs=re_slabs, im_slabs=im_slabs):
            sr, si = carry
            rows = pl.ds(t, nb, stride=pitch)
            br = jnp.concatenate([bu_ref[s, rows, :] for s in re_slabs], axis=1)
            bi = jnp.concatenate([bu_ref[s, rows, :] for s in im_slabs], axis=1)
            nr = ar * sr - ai * si + br
            ni = ar * si + ai * sr + bi
            out_rows = pl.ds(pl.multiple_of(t * nb, nb), nb)
            for k in range(half):
                xs_ref[re_slabs[k], out_rows, :] = nr[:, k * LANES:(k + 1) * LANES]
                xs_ref[im_slabs[k], out_rows, :] = ni[:, k * LANES:(k + 1) * LANES]
            return nr, ni

        sr, si = lax.fori_loop(0, tc, step, (st_ref[:, re_cols], st_ref[:, im_cols]), unroll=8)
        st_ref[:, re_cols] = sr
        st_ref[:, im_cols] = si

    for cb in range(SSM_BLOCKS):
        states = jnp.concatenate([xs_ref[cb * SSM_SLABS + k] for k in range(SSM_SLABS)], axis=1)
        yt_ref[cb] = _dot(states.astype(BF16), wc_ref[cb])
    cx = jnp.concatenate(
        [jnp.concatenate([yt_ref[cb, pl.ds(b, tc, stride=nb), :] for b in range(nb)], axis=0)
         for cb in range(SSM_BLOCKS)], axis=1)
    y = jax.nn.gelu(cx + d_ref[...] * u)
    gate = _dot(y.astype(BF16), wglu_ref[...])
    o_ref[...] = (y * jax.nn.sigmoid(gate)).astype(o_ref.dtype).reshape(nb, tc, SSM_WIDTH)


def _ssm_weights(a_re, a_im, log_dt, b_re, b_im, c_re, c_im):
    gpb = LANES // SSM_GROUP
    dt = jnp.exp(log_dt)[:, None]
    lr = jnp.minimum(a_re, -1e-4)
    li = a_im
    mag = jnp.exp(lr * dt)
    abr = mag * jnp.cos(li * dt)
    abi = mag * jnp.sin(li * dt)
    den = lr * lr + li * li
    qr = ((abr - 1.0) * lr + abi * li) / den
    qi = (abi * lr - (abr - 1.0) * li) / den
    bbr = b_re * qr[:, :, None] - b_im * qi[:, :, None]
    bbi = b_re * qi[:, :, None] + b_im * qr[:, :, None]
    eye = jnp.eye(gpb, dtype=F32)

    def pack_b(m):
        m4 = m.reshape(SSM_BLOCKS, gpb, SSM_STATE, SSM_GROUP)
        return jnp.einsum("kgpc,gh->kgchp", m4, eye).reshape(SSM_BLOCKS, LANES, SSM_BLOCK_STATE)

    def pack_c(m):
        m4 = m.reshape(SSM_BLOCKS, gpb, SSM_GROUP, SSM_STATE)
        return jnp.einsum("kgcp,gh->kgphc", m4, eye).reshape(SSM_BLOCKS, SSM_BLOCK_STATE, LANES)

    wb = jnp.concatenate([pack_b(bbr), pack_b(bbi)], axis=2).astype(BF16)
    wc = jnp.concatenate([pack_c(c_re), -pack_c(c_im)], axis=1).astype(BF16)
    ar = abr.reshape(SSM_BLOCKS, 1, SSM_BLOCK_STATE)
    ai = abi.reshape(SSM_BLOCKS, 1, SSM_BLOCK_STATE)
    return wb, wc, ar, ai


def _ssm(u, wb, wc, ar, ai, d, wglu):
    bsz, t, _ = u.shape
    tc = T_SSM
    pitch = tc + SUBLANES
    n_slabs = SSM_BLOCKS * SSM_SLABS
    blk = pl.BlockSpec((bsz, tc, SSM_WIDTH), lambda i: (0, i, 0))
    return pl.pallas_call(
        functools.partial(_ssm_kernel, tc=tc, nb=bsz, pitch=pitch),
        grid=(t // tc,),
        in_specs=[blk, _const_spec(wb.shape), _const_spec(wc.shape), _const_spec(ar.shape), _const_spec(ai.shape),
                  _const_spec(d.shape), _const_spec(wglu.shape)],
        out_specs=blk,
        out_shape=jax.ShapeDtypeStruct((bsz, t, SSM_WIDTH), BF16),
        scratch_shapes=[
            pltpu.VMEM((n_slabs, bsz * pitch, LANES), F32),
            pltpu.VMEM((n_slabs, tc * bsz, LANES), F32),
            pltpu.VMEM((SSM_BLOCKS, tc * bsz, LANES), F32),
            pltpu.VMEM((bsz, n_slabs * LANES), F32),
        ],
        compiler_params=_params("arbitrary"),
        name="ssm",
    )(u, wb, wc, ar, ai, d, wglu)


def _cmp_kernel(x_ref, pe_ref, w1_ref, w2_ref, o_ref, *, transpose_out):
    n = x_ref.shape[0] // CMP_STRIDE
    first = second = None
    for j in range(CMP_STRIDE):
        xj = x_ref[pl.ds(j, n, stride=CMP_STRIDE), :]
        a = _dot((xj + pe_ref[j:j + 1, :]).astype(BF16), w1_ref[0, j])
        b = _dot((xj + pe_ref[CMP_STRIDE + j:CMP_STRIDE + j + 1, :]).astype(BF16), w1_ref[1, j])
        first = a if first is None else first + a
        second = b if second is None else second + b
    hid = first + pltpu.roll(second, n - 1, 0)
    hb = jax.nn.gelu(hid).astype(BF16)
    for g in range(NSA_KV_HEADS):
        hg = hb[:, g * CMP_HIDDEN:(g + 1) * CMP_HIDDEN]
        if transpose_out:
            out = lax.dot_general(w2_ref[...], hg, (((1,), (1,)), ((), ())), preferred_element_type=F32)
        else:
            out = _dot(hg, w2_ref[...])
        o_ref[g] = out.astype(o_ref.dtype)


def _compress(kvc, which, pe, w1, w2, transpose_out):
    bsz, t, _ = kvc.shape
    n = t // CMP_STRIDE
    g = NSA_KV_HEADS
    pe2 = jnp.tile(pe, (1, g))
    w1r = w1.reshape(2, CMP_STRIDE, HEAD_DIM, CMP_HIDDEN)
    w1bd = jnp.einsum("hjdc,gk->hjgdkc", w1r, jnp.eye(g, dtype=F32)).reshape(2, CMP_STRIDE, g * HEAD_DIM, g * CMP_HIDDEN)
    w1bd = w1bd.astype(BF16)
    if transpose_out:
        w2k = w2.T.astype(BF16)
        out_block, out_shape = (None, g, HEAD_DIM, n), (bsz, g, HEAD_DIM, n)
    else:
        w2k = w2.astype(BF16)
        out_block, out_shape = (None, g, n, HEAD_DIM), (bsz, g, n, HEAD_DIM)
    return pl.pallas_call(
        functools.partial(_cmp_kernel, transpose_out=transpose_out),
        grid=(bsz,),
        in_specs=[
            pl.BlockSpec((None, t, KV_WIDTH), lambda b: (b, 0, which)),
            _const_spec(pe2.shape),
            _const_spec(w1bd.shape),
            _const_spec(w2k.shape),
        ],
        out_specs=pl.BlockSpec(out_block, lambda b: (b, 0, 0, 0)),
        out_shape=jax.ShapeDtypeStruct(out_shape, BF16),
        compiler_params=_params("parallel"),
        name="compress_v" if transpose_out else "compress_k",
    )(kvc, pe2, w1bd, w2k)


def _cmpattn_kernel(q_ref, kc_ref, vo_ref, qt_ref, oc_ref, *, tq, n_sub, n_slc, n_sel):
    cols = NSA_REP * tq
    kc = kc_ref[...]
    ncp = kc.shape[0]
    n_grp = SEL_PAD // SUBLANES
    row_in_grp = lax.broadcasted_iota(jnp.int32, (SUBLANES, tq), 0)

    def scores_of(sub):
        t0 = (pl.program_id(2) * n_sub + sub) * tq
        c0 = sub * cols
        qt = (q_ref[sub * tq:(sub + 1) * tq, :] * (HEAD_DIM ** -0.5 * LOG2E)).T.astype(BF16)
        qcat = jnp.concatenate([qt[r * HEAD_DIM:(r + 1) * HEAD_DIM, :] for r in range(NSA_REP)], axis=1)
        qt_ref[0:HEAD_DIM, c0:c0 + cols] = qcat
        t_idx = t0 + lax.broadcasted_iota(jnp.int32, (1, cols), 1) % tq
        last_visible = jnp.floor_divide(t_idx - (CMP_BLOCK - 1), CMP_STRIDE)
        cmask = lax.broadcasted_iota(jnp.int32, (ncp, cols), 0) <= last_visible
        s = jnp.where(cmask, _dot(kc, qcat), NEG)
        e = jnp.exp2(s - jnp.max(s, axis=0, keepdims=True))
        p = jnp.where(cmask, e * (1.0 / jnp.sum(e, axis=0, keepdims=True)), 0.0).astype(BF16)
        both = _dot(vo_ref[...], p)
        oc_ref[:, c0:c0 + cols] = both[0:HEAD_DIM, :]
        imp = both[HEAD_DIM:, 0:tq]
        for r in range(1, NSA_REP):
            imp = imp + both[HEAD_DIM:, r * tq:(r + 1) * tq]
        j_idx = lax.broadcasted_iota(jnp.int32, (SEL_PAD, tq), 0)
        cur = (t0 + lax.broadcasted_iota(jnp.int32, (SEL_PAD, tq), 1)) // SEL_BLOCK
        valid = j_idx <= cur
        forced = (j_idx == 0) | (j_idx == cur) | (j_idx == cur - 1)
        return jnp.where(valid, imp, -1.0) + jnp.where(forced, FORCE_BONUS, 0.0), valid

    def count_group(rank, score, jg):
        rank = list(rank)
        for j in range(jg * SUBLANES, min((jg + 1) * SUBLANES, n_slc)):
            row = jnp.broadcast_to(score[j:j + 1, :], (SUBLANES, tq))
            for g in range(n_grp):
                grp = score[SUBLANES * g:SUBLANES * (g + 1), :]
                if g < jg:
                    beats = row > grp
                elif g > jg:
                    beats = row >= grp
                else:
                    beats = (row > grp) | ((row == grp) & (row_in_grp > j % SUBLANES))
                rank[g] = rank[g] + jnp.where(beats, 1.0, 0.0)
        return tuple(rank)

    tiles = [scores_of(sub) for sub in range(n_sub)]
    zeros = tuple(jnp.zeros((SUBLANES, tq), F32) for _ in range(n_grp))
    ranks = [count_group(zeros, score, 0) for score, _ in tiles]
    for sub, (score, valid) in enumerate(tiles):
        t0 = (pl.program_id(2) * n_sub + sub) * tq
        last_grp = (t0 + tq - 1) // SEL_BLOCK // SUBLANES
        rank = ranks[sub]
        for jg in range(1, pl.cdiv(n_slc, SUBLANES)):
            rank = lax.cond(jg <= last_grp, functools.partial(count_group, score=score, jg=jg), lambda r: r, rank)
        sel = (jnp.concatenate(rank, axis=0) < n_sel) & valid
        selm1 = jnp.where(sel, 0.0, -1.0).astype(BF16)
        for r in range(NSA_REP):
            qt_ref[HEAD_DIM:K_PAD, sub * cols + r * tq:sub * cols + (r + 1) * tq] = selm1


def _cmpattn(q, kc, vo):
    bsz, t, _ = q.shape
    g = NSA_KV_HEADS
    tq = T_ATTN
    ncp = kc.shape[2]
    n_slc = t // SEL_BLOCK
    n_sub = ATTN_SUBTILES
    kern = functools.partial(_cmpattn_kernel, tq=tq, n_sub=n_sub, n_slc=n_slc, n_sel=min(N_SELECT, n_slc))
    return pl.pallas_call(
        kern,
        grid=(bsz, g, t // (n_sub * tq)),
        in_specs=[
            pl.BlockSpec((None, n_sub * tq, NSA_REP * HEAD_DIM), lambda b, h, i: (b, i, h)),
            pl.BlockSpec((None, None, ncp, HEAD_DIM), lambda b, h, i: (b, h, 0, 0)),
            pl.BlockSpec((None, None, HEAD_DIM + SEL_PAD, ncp), lambda b, h, i: (b, h, 0, 0)),
        ],
        out_specs=[
            pl.BlockSpec((None, None, K_PAD, n_sub * NSA_REP * tq), lambda b, h, i: (b, h, 0, i)),
            pl.BlockSpec((None, None, HEAD_DIM, n_sub * NSA_REP * tq), lambda b, h, i: (b, h, 0, i)),
        ],
        out_shape=[
            jax.ShapeDtypeStruct((bsz, g, K_PAD, NSA_REP * t), BF16),
            jax.ShapeDtypeStruct((bsz, g, HEAD_DIM, NSA_REP * t), F32),
        ],
        compiler_params=_params("parallel", "parallel", "parallel"),
        name="cmpattn",
    )(q, kc, vo)


def _overlap_t(t):
    ncp = t // CMP_STRIDE
    n_cmp = ncp - 1
    n_slc = t // SEL_BLOCK
    starts = np.arange(n_cmp) * CMP_STRIDE
    sel_starts = np.arange(n_slc) * SEL_BLOCK
    ov = ((starts[:, None] < sel_starts[None, :] + SEL_BLOCK)
          & (starts[:, None] + CMP_BLOCK > sel_starts[None, :])).astype(np.float32)
    ovp = np.zeros((ncp, SEL_PAD), np.float32)
    ovp[:n_cmp, :n_slc] = ov
    return jnp.asarray(ovp.T, dtype=BF16)


class _AttnTile:
    def __init__(self, qi, odd, qt, refs, bufs, tq):
        self.qi, self.odd = qi, odd
        self.qt, self.tq, self.tk = qt, tq, tq
        self.ks_ref, self.kw_ref, self.vs_ref, self.vw_ref, self.e_ref = refs
        self.s0_ref, self.s1_ref = bufs
        cols = NSA_REP * tq
        self.q_pos = qi * tq + lax.broadcasted_iota(jnp.int32, (1, cols), 1) % tq
        self.k_off = lax.broadcasted_iota(jnp.int32, (tq, cols), 0)
        self.ones_rows = jnp.where(lax.broadcasted_iota(jnp.int32, (BF16_ROWS, tq), 0) == 0, 1.0, 0.0).astype(BF16)
        self.init = (jnp.full((1, cols), NEG, F32), jnp.zeros((V_ROWS, cols), F32))

    def tile(self, kt):
        return pl.ds(pl.multiple_of(kt * self.tk, self.tk), self.tk)

    def sel_scores(self, kt):
        return _dot(self.ks_ref[self.tile(kt), :] + self.e_ref[self.tile(kt), :], self.qt)

    def win_scores(self, kt):
        return _dot(self.kw_ref[self.tile(kt), :], self.qt)

    def update(self, carry, scores, v_ref, kt):
        m, acc = carry
        m_new = jnp.maximum(m, jnp.max(scores(), axis=0, keepdims=True))
        p = jnp.exp2(scores() - m_new).astype(BF16)
        vt = jnp.concatenate([v_ref[:, self.tile(kt)], self.ones_rows], axis=0)
        return m_new, jnp.exp2(m - m_new) * acc + _dot(vt, p)

    @staticmethod
    def normalise(carry):
        _, acc = carry
        return acc[0:HEAD_DIM, :] * (1.0 / acc[HEAD_DIM:HEAD_DIM + 1, :])

    def causal(self, kt, s):
        return jnp.where(self.k_off <= self.q_pos - kt * self.tk, s, NEG)

    def start(self):
        self.s0_ref[...] = self.sel_scores(0)

    def sel_pair(self, j, carry):
        self.s1_ref[...] = self.sel_scores(2 * j + 1)
        carry = self.update(carry, lambda: self.s0_ref[...], self.vs_ref, 2 * j)
        self.s0_ref[...] = self.sel_scores(2 * j + 2)
        return self.update(carry, lambda: self.s1_ref[...], self.vs_ref, 2 * j + 1)

    def tail(self):
        qi, tk = self.qi, self.tk
        n_back = WINDOW // tk
        assert n_back % 2 == 0
        even = 2 * (qi // 2)

        def win_masked(back):
            kt = jnp.maximum(qi - back, 0)
            s = self.win_scores(kt)
            if back == 0:
                return kt, self.causal(qi, s)
            if back == n_back:
                oldest_visible = jnp.where(qi >= back, self.q_pos - kt * tk - WINDOW, tk)
                return kt, jnp.where(self.k_off > oldest_visible, s, NEG)
            return kt, jnp.where(qi >= back, s, NEG)

        win_carry = self.init
        pending = None
        for n, back in enumerate(range(n_back, -1, -1)):
            buf = (self.s1_ref, self.s0_ref)[n % 2]
            kt, s = win_masked(back)
            buf[...] = s
            if pending is None:
                held = (lambda: self.s0_ref[...]) if self.odd else (lambda: self.causal(even, self.s0_ref[...]))
                self.sel_carry = self.update(self.sel_carry, held, self.vs_ref, even)
            else:
                win_carry = self.update(win_carry, lambda b=pending[0]: b[...], self.vw_ref, pending[1])
            pending = (buf, kt)
        if self.odd:
            self.s0_ref[...] = self.causal(qi, self.sel_scores(qi))
        self.o_win = self.normalise(self.update(win_carry, lambda: pending[0][...], self.vw_ref, pending[1]))
        if self.odd:
            self.sel_carry = self.update(self.sel_carry, lambda: self.s0_ref[...], self.vs_ref, qi)
        self.o_sel = self.normalise(self.sel_carry)


def _attn_kernel(qt_ref, ks_ref, kw_ref, vs_ref, vw_ref, e_ref, oc_ref, gl_ref, o_ref, *bufs, tq, n_sub):
    cols = NSA_REP * tq
    refs = (ks_ref, kw_ref, vs_ref, vw_ref, e_ref)
    assert n_sub % 2 == 0
    step = pl.program_id(2)
    tiles = [_AttnTile(step * n_sub + sub, sub % 2 == 1, qt_ref[:, sub * cols:(sub + 1) * cols], refs,
                       bufs[2 * sub:2 * sub + 2], tq) for sub in range(n_sub)]
    for tl in tiles:
        tl.start()
    common = step * (n_sub // 2)
    joint = lambda j, carries: tuple(tl.sel_pair(j, c) for tl, c in zip(tiles, carries))
    carries = lax.fori_loop(0, common, joint, tuple(tl.init for tl in tiles))
    for sub, (tl, carry) in enumerate(zip(tiles, carries)):
        for k in range(sub // 2):
            carry = tl.sel_pair(common + k, carry)
        tl.sel_carry = carry
    for tl in tiles:
        tl.tail()

    for sub, tl in enumerate(tiles):
        gates = jax.nn.sigmoid(gl_ref[:, sub * tq:(sub + 1) * tq])
        oc = oc_ref[:, sub * cols:(sub + 1) * cols]
        outs = []
        for r in range(NSA_REP):
            c = slice(r * tq, (r + 1) * tq)
            outs.append(gates[r:r + 1, :] * oc[:, c]
                        + gates[NSA_REP + r:NSA_REP + r + 1, :] * tl.o_sel[:, c]
                        + gates[2 * NSA_REP + r:2 * NSA_REP + r + 1, :] * tl.o_win[:, c])
        o_ref[sub * tq:(sub + 1) * tq, :] = jnp.concatenate(outs, axis=0).T.astype(o_ref.dtype)


def _attn(qt, ks, kw, vst, vwt, e, oc, glt):
    bsz, t, _ = ks.shape
    g = NSA_KV_HEADS
    tq = T_ATTN
    cols = NSA_REP * tq
    full_k = pl.BlockSpec((None, t, K_PAD), lambda b, h, i: (b, 0, h))
    full_v = pl.BlockSpec((None, HEAD_DIM, t), lambda b, h, i: (b, h, 0))
    n_sub = ATTN_SUBTILES
    return pl.pallas_call(
        functools.partial(_attn_kernel, tq=tq, n_sub=n_sub),
        grid=(bsz, g, t // (n_sub * tq)),
        in_specs=[
            pl.BlockSpec((None, None, K_PAD, n_sub * cols), lambda b, h, i: (b, h, 0, i)),
            full_k, full_k, full_v, full_v,
            _const_spec(e.shape),
            pl.BlockSpec((None, None, HEAD_DIM, n_sub * cols), lambda b, h, i: (b, h, 0, i)),
            pl.BlockSpec((None, None, 4 * NSA_REP, n_sub * tq), lambda b, h, i: (b, h, 0, i)),
        ],
        out_specs=pl.BlockSpec((None, n_sub * tq, NSA_REP * HEAD_DIM), lambda b, h, i: (b, i, h)),
        out_shape=jax.ShapeDtypeStruct((bsz, t, NSA_WIDTH), BF16),
        scratch_shapes=[pltpu.VMEM((tq, cols), F32)] * (2 * n_sub),
        compiler_params=_params("parallel", "parallel", "arbitrary"),
        name="attn",
    )(qt, ks, kw, vst, vwt, e, oc, glt)


def _mix_kernel(x_ref, ys_ref, yn_ref, g_ref, wm_ref, wbs_ref, wbn_ref, wo_ref, h_ref):
    x = x_ref[...]
    n1 = _rms(x, g_ref[...]).astype(BF16)
    logits = _dot(n1, wm_ref[...])
    a = _dot(ys_ref[...], wbs_ref[...])
    b = _dot(yn_ref[...], wbn_ref[...])
    mixed = jax.nn.sigmoid(logits[:, :D_MODEL]) * a + jax.nn.sigmoid(logits[:, D_MODEL:]) * b
    h_ref[...] = x + _dot(mixed.astype(BF16), wo_ref[...])


def _mix(x, ys, yn, g, wm, wbs, wbn, wo):
    bsz, t, d = x.shape
    tm = TM_MIX
    row = lambda w: pl.BlockSpec((None, tm, w), lambda b, i: (b, i, 0))
    return pl.pallas_call(
        _mix_kernel,
        grid=(bsz, t // tm),
        in_specs=[row(d), row(SSM_WIDTH), row(NSA_WIDTH), _const_spec(g.shape), _const_spec(wm.shape),
                  _const_spec(wbs.shape), _const_spec(wbn.shape), _const_spec(wo.shape)],
        out_specs=row(d),
        out_shape=jax.ShapeDtypeStruct((bsz, t, d), F32),
        compiler_params=_params("parallel", "parallel"),
        name="mix",
    )(x, ys, yn, g, wm, wbs, wbn, wo)


def _ffn_kernel(h_ref, p_ref, gf_ref, wu_ref, cw_ref, cbias_ref, wd_ref,
                gp_ref, wg_ref, wp_ref, gl_ref, o_ref, ca_ref, cb_ref, act_ref, *, tm):
    first_tile = pl.program_id(1) == 0
    h = h_ref[...]
    n2 = _rms(h, gf_ref[...]).astype(BF16)

    def cols(c, half):
        return slice(half * D_FF + c * FF_CHUNK, half * D_FF + (c + 1) * FF_CHUNK)

    def conv(hid, buf_ref, c, cw, cb):
        buf_ref[c, 0:SUBLANES, :] = buf_ref[c, tm:tm + SUBLANES, :]
        buf_ref[c, SUBLANES:SUBLANES + tm, :] = hid
        x1 = buf_ref[c, pl.ds(SUBLANES - 1, tm), :]
        x2 = buf_ref[c, pl.ds(SUBLANES - 2, tm), :]
        return cw[0:1, :] * x2 + cw[1:2, :] * x1 + cw[2:3, :] * hid + cb

    @pl.when(first_tile)
    def _():
        ca_ref[:, tm:tm + SUBLANES, :] = jnp.zeros((N_FF_CHUNKS, SUBLANES, FF_CHUNK), F32)
        cb_ref[:, tm:tm + SUBLANES, :] = jnp.zeros((N_FF_CHUNKS, SUBLANES, FF_CHUNK), F32)

    up = lambda c: (_dot(n2, wu_ref[:, cols(c, 0)]), _dot(n2, wu_ref[:, cols(c, 1)]))
    nxt = up(0)
    acc = None
    for c in range(N_FF_CHUNKS):
        ha, hb = nxt
        if c + 1 < N_FF_CHUNKS:
            nxt = up(c + 1)
        a = conv(ha, ca_ref, c, cw_ref[:, cols(c, 0)], cbias_ref[:, cols(c, 0)])
        b = conv(hb, cb_ref, c, cw_ref[:, cols(c, 1)], cbias_ref[:, cols(c, 1)])
        act_ref[:, c * FF_CHUNK:(c + 1) * FF_CHUNK] = (jax.nn.gelu(a) * b).astype(BF16)
        if (c + 1) % FF_DOWN_GROUP == 0 or c + 1 == N_FF_CHUNKS:
            lo = (c // FF_DOWN_GROUP) * FF_DOWN_GROUP * FF_CHUNK
            down = _dot(act_ref[:, lo:(c + 1) * FF_CHUNK], wd_ref[lo:(c + 1) * FF_CHUNK, :])
            acc = down if acc is None else acc + down
    h2 = h + acc
    gate = jax.nn.sigmoid(_dot(_rms(h2, gp_ref[...]).astype(BF16), wg_ref[...]))
    h3 = h2 + gate * _dot(p_ref[...].astype(BF16), wp_ref[...])
    o_ref[...] = _rms(h3, gl_ref[...])


def _ffn(h, p, gf, w_up, conv_w, conv_b, w_down, gp, wg, wp, gl):
    bsz, t, d = h.shape
    tm = TM_FFN
    nc, ch = N_FF_CHUNKS, FF_CHUNK

    row = lambda w: pl.BlockSpec((None, tm, w), lambda b, i: (b, i, 0))
    consts = [gf, w_up.astype(BF16), conv_w, conv_b.reshape(1, 2 * D_FF), w_down.astype(BF16), gp, wg, wp, gl]
    return pl.pallas_call(
        functools.partial(_ffn_kernel, tm=tm),
        grid=(bsz, t // tm),
        in_specs=[row(d), row(PLE_DIM)] + [_const_spec(c.shape) for c in consts],
        out_specs=row(d),
        out_shape=jax.ShapeDtypeStruct((bsz, t, d), F32),
        scratch_shapes=[
            pltpu.VMEM((nc, SUBLANES + tm, ch), F32),
            pltpu.VMEM((nc, SUBLANES + tm, ch), F32),
            pltpu.VMEM((tm, D_FF), BF16),
        ],
        compiler_params=_params("arbitrary", "arbitrary"),
        name="ffn",
    )(h, p, *consts)


def _layer(x, p, g_mix, w_in, ssm_a_re, ssm_a_im, ssm_log_dt, ssm_b_re, ssm_b_im, ssm_c_re, ssm_c_im,
           ssm_d, ssm_w_glu, cmp_pe_k, cmp_pe_v, cmp_wk1, cmp_wk2, cmp_wv1, cmp_wv2, w_br_ssm, w_br_nsa,
           w_out, g_ffn, w_up, conv_w, conv_b, w_down, g_ple, w_ple_gate, w_ple_proj, g_out):
    bsz, t, d = x.shape
    g = NSA_KV_HEADS
    assert d == D_MODEL and t % (ATTN_SUBTILES * T_ATTN) == 0 and t // SEL_BLOCK <= SEL_PAD

    u, q, kvc, ks, kw, vs, vw, gate_logits = _inproj(x, g_mix.reshape(1, d), _inproj_weight(w_in))

    wb, wc, ar, ai = _ssm_weights(ssm_a_re, ssm_a_im, ssm_log_dt, ssm_b_re, ssm_b_im, ssm_c_re, ssm_c_im)
    y_ssm = _ssm(u, wb, wc, ar, ai, ssm_d.reshape(1, SSM_WIDTH), ssm_w_glu.astype(BF16))

    kc = _compress(kvc, 0, cmp_pe_k, cmp_wk1, cmp_wk2, transpose_out=False)
    vct = _compress(kvc, 1, cmp_pe_v, cmp_wv1, cmp_wv2, transpose_out=True)
    vo = jnp.concatenate([vct, jnp.broadcast_to(_overlap_t(t), (bsz, g, SEL_PAD, t // CMP_STRIDE))], axis=2)
    qt, o_cmp = _cmpattn(q, kc, vo)

    blk_of_key = np.arange(t)[:, None] // SEL_BLOCK == np.arange(SEL_PAD)[None, :]
    sel_cols = np.concatenate([np.zeros((t, HEAD_DIM)), np.where(blk_of_key, SEL_BIAS, 0.0)], axis=1)
    glt = gate_logits[:, :, :3 * NSA_HEADS].reshape(bsz, t, 3, g, NSA_REP).transpose(0, 3, 2, 4, 1)
    glt = jnp.pad(glt.reshape(bsz, g, 3 * NSA_REP, t), ((0, 0), (0, 0), (0, NSA_REP), (0, 0)))
    y_nsa = _attn(qt, ks, kw, vs.transpose(0, 2, 1), vw.transpose(0, 2, 1), jnp.asarray(sel_cols, dtype=BF16),
                  o_cmp, glt)

    h = _mix(x, y_ssm, y_nsa, g_mix.reshape(1, d), w_in[:, OFF_MERGE:].astype(BF16), w_br_ssm.astype(BF16),
             w_br_nsa.astype(BF16), w_out.astype(BF16))
    return _ffn(h, p, g_ffn.reshape(1, d), w_up, conv_w, conv_b, w_down, g_ple.reshape(1, d),
                w_ple_gate.astype(BF16), w_ple_proj.astype(BF16), g_out.reshape(1, d))


def kernel(x, p, g_mix, w_in, ssm_a_re, ssm_a_im, ssm_log_dt, ssm_b_re, ssm_b_im, ssm_c_re, ssm_c_im, ssm_d, ssm_w_glu, cmp_pe_k, cmp_pe_v, cmp_wk1, cmp_wk2, cmp_wv1, cmp_wv2, w_br_ssm, w_br_nsa, w_out, g_ffn, w_up, conv_w, conv_b, w_down, g_ple, w_ple_gate, w_ple_proj, g_final):
    assert p.shape[0] == 1, "one trunk layer"
    return _layer(x, p[0], g_mix[0], w_in[0], ssm_a_re[0], ssm_a_im[0], ssm_log_dt[0], ssm_b_re[0], ssm_b_im[0],
                  ssm_c_re[0], ssm_c_im[0], ssm_d[0], ssm_w_glu[0], cmp_pe_k[0], cmp_pe_v[0], cmp_wk1[0],
                  cmp_wk2[0], cmp_wv1[0], cmp_wv2[0], w_br_ssm[0], w_br_nsa[0], w_out[0], g_ffn[0], w_up[0],
                  conv_w[0], conv_b[0], w_down[0], g_ple[0], w_ple_gate[0], w_ple_proj[0], g_final)
```

```python
import functools

import numpy as np
import jax
import jax.numpy as jnp
from jax import lax
from jax.experimental import pallas as pl
from jax.experimental.pallas import tpu as pltpu

F32 = jnp.float32
BF16 = jnp.bfloat16

D_MODEL = 1024
SSM_WIDTH = 512
SSM_GROUP = 16
SSM_GROUPS = SSM_WIDTH // SSM_GROUP
SSM_STATE = 64
NSA_HEADS = 8
NSA_KV_HEADS = 2
NSA_REP = NSA_HEADS // NSA_KV_HEADS
HEAD_DIM = 64
NSA_WIDTH = NSA_HEADS * HEAD_DIM
KV_WIDTH = NSA_KV_HEADS * HEAD_DIM
CMP_BLOCK = 32
CMP_STRIDE = 16
CMP_HIDDEN = 128
SEL_BLOCK = 64
N_SELECT = 16
WINDOW = 512
D_FF = 2816
PLE_DIM = 256
EPS = 1e-6
NEG = -1e30
FORCE_BONUS = 100.0

OFF_Q = SSM_WIDTH
OFF_KV = OFF_Q + NSA_WIDTH
OFF_NSA_GATE = OFF_KV + 6 * KV_WIDTH
OFF_MERGE = OFF_NSA_GATE + 3 * NSA_HEADS

LANES = 128
SUBLANES = 8
BF16_ROWS = 16
VMEM_LIMIT = 56 * 1024 * 1024
GATE_PAD = LANES
SEL_PAD = 64
K_PAD = HEAD_DIM + SEL_PAD
V_ROWS = HEAD_DIM + BF16_ROWS
SEL_BIAS = 2.0 ** 100
LOG2E = 1.4426950408889634
SSM_BLOCKS = SSM_WIDTH // LANES
SSM_BLOCK_STATE = (LANES // SSM_GROUP) * SSM_STATE
SSM_SLABS = 2 * SSM_BLOCK_STATE // LANES
FF_CHUNK = 256
N_FF_CHUNKS = D_FF // FF_CHUNK
FF_DOWN_GROUP = 4

TM_INPROJ = 512
TM_MIX = 1024
TM_FFN = 512
T_SSM = 64
T_ATTN = 2 * LANES
ATTN_SUBTILES = 4


def _rms(x, g):
    return x * lax.rsqrt(jnp.mean(x * x, axis=-1, keepdims=True) + EPS) * g


def _dot(a, b):
    return jnp.dot(a, b, preferred_element_type=F32)


def _params(*sem):
    return pltpu.CompilerParams(dimension_semantics=sem, vmem_limit_bytes=VMEM_LIMIT)


def _const_spec(shape):
    zeros = (0,) * len(shape)
    return pl.BlockSpec(shape, lambda *_: zeros, pipeline_mode=pl.Buffered(1))


_INPROJ_WIDTHS = (SSM_WIDTH, NSA_WIDTH, 2 * KV_WIDTH, NSA_KV_HEADS * K_PAD, NSA_KV_HEADS * K_PAD,
                  KV_WIDTH, KV_WIDTH, GATE_PAD)
_INPROJ_DTYPES = (F32, F32, F32, BF16, BF16, BF16, BF16, F32)


def _inproj_kernel(x_ref, g_ref, w_ref, *out_refs):
    n1 = _rms(x_ref[...], g_ref[...]).astype(BF16)
    z = _dot(n1, w_ref[...])
    off = 0
    for ref, width in zip(out_refs, _INPROJ_WIDTHS):
        ref[...] = z[:, off:off + width].astype(ref.dtype)
        off += width


def _inproj_weight(w_in):
    d = w_in.shape[0]
    wkv = w_in[:, OFF_KV:OFF_NSA_GATE].reshape(d, 6, NSA_KV_HEADS, HEAD_DIM)
    flat = lambda w: w.reshape(d, KV_WIDTH)
    padk = lambda w: jnp.pad(w, ((0, 0), (0, 0), (0, SEL_PAD))).reshape(d, NSA_KV_HEADS * K_PAD)
    gates = jnp.pad(w_in[:, OFF_NSA_GATE:OFF_MERGE], ((0, 0), (0, GATE_PAD - 3 * NSA_HEADS)))
    return jnp.concatenate([w_in[:, :OFF_KV], flat(wkv[:, 0]), flat(wkv[:, 1]), padk(wkv[:, 2]), padk(wkv[:, 4]),
                            flat(wkv[:, 3]), flat(wkv[:, 5]), gates], axis=1).astype(BF16)


def _inproj(x, g, w):
    bsz, t, d = x.shape
    tm = TM_INPROJ
    row = lambda width: pl.BlockSpec((None, tm, width), lambda b, i: (b, i, 0))
    return pl.pallas_call(
        _inproj_kernel,
        grid=(bsz, t // tm),
        in_specs=[row(d), _const_spec((1, d)), _const_spec(w.shape)],
        out_specs=[row(width) for width in _INPROJ_WIDTHS],
        out_shape=[jax.ShapeDtypeStruct((bsz, t, width), dt) for width, dt in zip(_INPROJ_WIDTHS, _INPROJ_DTYPES)],
        compiler_params=_params("parallel", "parallel"),
        name="inproj",
    )(x, g, w)


def _ssm_kernel(u_ref, wb_ref, wc_ref, ar_ref, ai_ref, d_ref, wglu_ref, o_ref, bu_ref, xs_ref, yt_ref, st_ref,
                *, tc, nb, pitch):
    @pl.when(pl.program_id(0) == 0)
    def _():
        st_ref[...] = jnp.zeros_like(st_ref)

    u = u_ref[...].reshape(nb * tc, SSM_WIDTH)
    ub = u.astype(BF16)
    half = SSM_SLABS // 2
    for cb in range(SSM_BLOCKS):
        res = _dot(ub[:, cb * LANES:(cb + 1) * LANES], wb_ref[cb])
        for k in range(SSM_SLABS):
            for b in range(nb):
                bu_ref[cb * SSM_SLABS + k, b * pitch:b * pitch + tc, :] = res[b * tc:(b + 1) * tc, k * LANES:(k + 1) * LANES]

    for cb in range(SSM_BLOCKS):
        w2 = 2 * SSM_BLOCK_STATE
        re_cols = slice(cb * w2, cb * w2 + SSM_BLOCK_STATE)
        im_cols = slice(cb * w2 + SSM_BLOCK_STATE, (cb + 1) * w2)
        ar = jnp.broadcast_to(ar_ref[cb], (nb, SSM_BLOCK_STATE))
        ai = jnp.broadcast_to(ai_ref[cb], (nb, SSM_BLOCK_STATE))
        re_slabs = [cb * SSM_SLABS + k for k in range(half)]
        im_slabs = [cb * SSM_SLABS + half + k for k in range(half)]

        def step(t, carry, ar=ar, ai=ai, re_slabs=re_slabs, im_slabs=im_slabs):
            sr, si = carry
            rows = pl.ds(t, nb, stride=pitch)
            br = jnp.concatenate([bu_ref[s, rows, :] for s in re_slabs], axis=1)
            bi = jnp.concatenate([bu_ref[s, rows, :] for s in im_slabs], axis=1)
            nr = ar * sr - ai * si + br
            ni = ar * si + ai * sr + bi
            out_rows = pl.ds(pl.multiple_of(t * nb, nb), nb)
            for k in range(half):
                xs_ref[re_slabs[k], out_rows, :] = nr[:, k * LANES:(k + 1) * LANES]
                xs_ref[im_slabs[k], out_rows, :] = ni[:, k * LANES:(k + 1) * LANES]
            return nr, ni

        sr, si = lax.fori_loop(0, tc, step, (st_ref[:, re_cols], st_ref[:, im_cols]), unroll=True)
        st_ref[:, re_cols] = sr
        st_ref[:, im_cols] = si
        states = jnp.concatenate([xs_ref[cb * SSM_SLABS + k] for k in range(SSM_SLABS)], axis=1)
        yt_ref[cb] = _dot(states.astype(BF16), wc_ref[cb])

    cx = jnp.concatenate(
        [jnp.concatenate([yt_ref[cb, pl.ds(b, tc, stride=nb), :] for b in range(nb)], axis=0)
         for cb in range(SSM_BLOCKS)], axis=1)
    y = jax.nn.gelu(cx + d_ref[...] * u)
    gate = _dot(y.astype(BF16), wglu_ref[...])
    o_ref[...] = (y * jax.nn.sigmoid(gate)).astype(o_ref.dtype).reshape(nb, tc, SSM_WIDTH)


def _ssm_weights(a_re, a_im, log_dt, b_re, b_im, c_re, c_im):
    gpb = LANES // SSM_GROUP
    dt = jnp.exp(log_dt)[:, None]
    lr = jnp.minimum(a_re, -1e-4)
    li = a_im
    mag = jnp.exp(lr * dt)
    abr = mag * jnp.cos(li * dt)
    abi = mag * jnp.sin(li * dt)
    den = lr * lr + li * li
    qr = ((abr - 1.0) * lr + abi * li) / den
    qi = (abi * lr - (abr - 1.0) * li) / den
    bbr = b_re * qr[:, :, None] - b_im * qi[:, :, None]
    bbi = b_re * qi[:, :, None] + b_im * qr[:, :, None]
    eye = jnp.eye(gpb, dtype=F32)

    def pack_b(m):
        m4 = m.reshape(SSM_BLOCKS, gpb, SSM_STATE, SSM_GROUP)
        return jnp.einsum("kgpc,gh->kgchp", m4, eye).reshape(SSM_BLOCKS, LANES, SSM_BLOCK_STATE)

    def pack_c(m):
        m4 = m.reshape(SSM_BLOCKS, gpb, SSM_GROUP, SSM_STATE)
        return jnp.einsum("kgcp,gh->kgphc", m4, eye).reshape(SSM_BLOCKS, SSM_BLOCK_STATE, LANES)

    wb = jnp.concatenate([pack_b(bbr), pack_b(bbi)], axis=2).astype(BF16)
    wc = jnp.concatenate([pack_c(c_re), -pack_c(c_im)], axis=1).astype(BF16)
    ar = abr.reshape(SSM_BLOCKS, 1, SSM_BLOCK_STATE)
    ai = abi.reshape(SSM_BLOCKS, 1, SSM_BLOCK_STATE)
    return wb, wc, ar, ai


def _ssm(u, wb, wc, ar, ai, d, wglu):
    bsz, t, _ = u.shape
    tc = T_SSM
    pitch = tc + SUBLANES
    n_slabs = SSM_BLOCKS * SSM_SLABS
    blk = pl.BlockSpec((bsz, tc, SSM_WIDTH), lambda i: (0, i, 0))
    return pl.pallas_call(
        functools.partial(_ssm_kernel, tc=tc, nb=bsz, pitch=pitch),
        grid=(t // tc,),
        in_specs=[blk, _const_spec(wb.shape), _const_spec(wc.shape), _const_spec(ar.shape), _const_spec(ai.shape),
                  _const_spec(d.shape), _const_spec(wglu.shape)],
        out_specs=blk,
        out_shape=jax.ShapeDtypeStruct((bsz, t, SSM_WIDTH), BF16),
        scratch_shapes=[
            pltpu.VMEM((n_slabs, bsz * pitch, LANES), F32),
            pltpu.VMEM((n_slabs, tc * bsz, LANES), F32),
            pltpu.VMEM((SSM_BLOCKS, tc * bsz, LANES), F32),
            pltpu.VMEM((bsz, n_slabs * LANES), F32),
        ],
        compiler_params=_params("arbitrary"),
        name="ssm",
    )(u, wb, wc, ar, ai, d, wglu)


def _cmp_kernel(x_ref, pe_ref, w1_ref, w2_ref, o_ref, *, transpose_out):
    n = x_ref.shape[0] // CMP_STRIDE
    first = second = None
    for j in range(CMP_STRIDE):
        xj = x_ref[pl.ds(j, n, stride=CMP_STRIDE), :]
        a = _dot((xj + pe_ref[j:j + 1, :]).astype(BF16), w1_ref[0, j])
        b = _dot((xj + pe_ref[CMP_STRIDE + j:CMP_STRIDE + j + 1, :]).astype(BF16), w1_ref[1, j])
        first = a if first is None else first + a
        second = b if second is None else second + b
    hid = first + pltpu.roll(second, n - 1, 0)
    hb = jax.nn.gelu(hid).astype(BF16)
    for g in range(NSA_KV_HEADS):
        hg = hb[:, g * CMP_HIDDEN:(g + 1) * CMP_HIDDEN]
        if transpose_out:
            out = lax.dot_general(w2_ref[...], hg, (((1,), (1,)), ((), ())), preferred_element_type=F32)
        else:
            out = _dot(hg, w2_ref[...])
        o_ref[g] = out.astype(o_ref.dtype)


def _compress(kvc, which, pe, w1, w2, transpose_out):
    bsz, t, _ = kvc.shape
    n = t // CMP_STRIDE
    g = NSA_KV_HEADS
    pe2 = jnp.tile(pe, (1, g))
    w1r = w1.reshape(2, CMP_STRIDE, HEAD_DIM, CMP_HIDDEN)
    w1bd = jnp.einsum("hjdc,gk->hjgdkc", w1r, jnp.eye(g, dtype=F32)).reshape(2, CMP_STRIDE, g * HEAD_DIM, g * CMP_HIDDEN)
    w1bd = w1bd.astype(BF16)
    if transpose_out:
        w2k = w2.T.astype(BF16)
        out_block, out_shape = (None, g, HEAD_DIM, n), (bsz, g, HEAD_DIM, n)
    else:
        w2k = w2.astype(BF16)
        out_block, out_shape = (None, g, n, HEAD_DIM), (bsz, g, n, HEAD_DIM)
    return pl.pallas_call(
        functools.partial(_cmp_kernel, transpose_out=transpose_out),
        grid=(bsz,),
        in_specs=[
            pl.BlockSpec((None, t, KV_WIDTH), lambda b: (b, 0, which)),
            _const_spec(pe2.shape),
            _const_spec(w1bd.shape),
            _const_spec(w2k.shape),
        ],
        out_specs=pl.BlockSpec(out_block, lambda b: (b, 0, 0, 0)),
        out_shape=jax.ShapeDtypeStruct(out_shape, BF16),
        compiler_params=_params("parallel"),
        name="compress_v" if transpose_out else "compress_k",
    )(kvc, pe2, w1bd, w2k)


def _cmpattn_kernel(q_ref, kc_ref, vo_ref, qt_ref, oc_ref, *, tq, n_sub, n_slc, n_sel):
    cols = NSA_REP * tq
    kc = kc_ref[...]
    ncp = kc.shape[0]
    n_grp = SEL_PAD // SUBLANES
    row_in_grp = lax.broadcasted_iota(jnp.int32, (SUBLANES, tq), 0)

    def scores_of(sub):
        t0 = (pl.program_id(2) * n_sub + sub) * tq
        c0 = sub * cols
        qt = (q_ref[sub * tq:(sub + 1) * tq, :] * (HEAD_DIM ** -0.5 * LOG2E)).T.astype(BF16)
        qcat = jnp.concatenate([qt[r * HEAD_DIM:(r + 1) * HEAD_DIM, :] for r in range(NSA_REP)], axis=1)
        qt_ref[0:HEAD_DIM, c0:c0 + cols] = qcat
        t_idx = t0 + lax.broadcasted_iota(jnp.int32, (1, cols), 1) % tq
        last_visible = jnp.floor_divide(t_idx - (CMP_BLOCK - 1), CMP_STRIDE)
        cmask = lax.broadcasted_iota(jnp.int32, (ncp, cols), 0) <= last_visible
        s = jnp.where(cmask, _dot(kc, qcat), NEG)
        e = jnp.exp2(s - jnp.max(s, axis=0, keepdims=True))
        p = jnp.where(cmask, e * (1.0 / jnp.sum(e, axis=0, keepdims=True)), 0.0).astype(BF16)
        both = _dot(vo_ref[...], p)
        oc_ref[:, c0:c0 + cols] = both[0:HEAD_DIM, :]
        imp = both[HEAD_DIM:, 0:tq]
        for r in range(1, NSA_REP):
            imp = imp + both[HEAD_DIM:, r * tq:(r + 1) * tq]
        j_idx = lax.broadcasted_iota(jnp.int32, (SEL_PAD, tq), 0)
        cur = (t0 + lax.broadcasted_iota(jnp.int32, (SEL_PAD, tq), 1)) // SEL_BLOCK
        valid = j_idx <= cur
        forced = (j_idx == 0) | (j_idx == cur) | (j_idx == cur - 1)
        return jnp.where(valid, imp, -1.0) + jnp.where(forced, FORCE_BONUS, 0.0), valid

    def count_group(rank, score, jg):
        rank = list(rank)
        for j in range(jg * SUBLANES, min((jg + 1) * SUBLANES, n_slc)):
            row = jnp.broadcast_to(score[j:j + 1, :], (SUBLANES, tq))
            for g in range(n_grp):
                grp = score[SUBLANES * g:SUBLANES * (g + 1), :]
                if g < jg:
                    beats = row > grp
                elif g > jg:
                    beats = row >= grp
                else:
                    beats = (row > grp) | ((row == grp) & (row_in_grp > j % SUBLANES))
                rank[g] = rank[g] + jnp.where(beats, 1.0, 0.0)
        return tuple(rank)

    tiles = [scores_of(sub) for sub in range(n_sub)]
    zeros = tuple(jnp.zeros((SUBLANES, tq), F32) for _ in range(n_grp))
    ranks = [count_group(zeros, score, 0) for score, _ in tiles]
    for sub, (score, valid) in enumerate(tiles):
        t0 = (pl.program_id(2) * n_sub + sub) * tq
        last_grp = (t0 + tq - 1) // SEL_BLOCK // SUBLANES
        rank = ranks[sub]
        for jg in range(1, pl.cdiv(n_slc, SUBLANES)):
            rank = lax.cond(jg <= last_grp, functools.partial(count_group, score=score, jg=jg), lambda r: r, rank)
        sel = (jnp.concatenate(rank, axis=0) < n_sel) & valid
        selm1 = jnp.where(sel, 0.0, -1.0).astype(BF16)
        for r in range(NSA_REP):
            qt_ref[HEAD_DIM:K_PAD, sub * cols + r * tq:sub * cols + (r + 1) * tq] = selm1


def _cmpattn(q, kc, vo):
    bsz, t, _ = q.shape
    g = NSA_KV_HEADS
    tq = T_ATTN
    ncp = kc.shape[2]
    n_slc = t // SEL_BLOCK
    n_sub = ATTN_SUBTILES
    kern = functools.partial(_cmpattn_kernel, tq=tq, n_sub=n_sub, n_slc=n_slc, n_sel=min(N_SELECT, n_slc))
    return pl.pallas_call(
        kern,
        grid=(bsz, g, t // (n_sub * tq)),
        in_specs=[
            pl.BlockSpec((None, n_sub * tq, NSA_REP * HEAD_DIM), lambda b, h, i: (b, i, h)),
            pl.BlockSpec((None, None, ncp, HEAD_DIM), lambda b, h, i: (b, h, 0, 0)),
            pl.BlockSpec((None, None, HEAD_DIM + SEL_PAD, ncp), lambda b, h, i: (b, h, 0, 0)),
        ],
        out_specs=[
            pl.BlockSpec((None, None, K_PAD, n_sub * NSA_REP * tq), lambda b, h, i: (b, h, 0, i)),
            pl.BlockSpec((None, None, HEAD_DIM, n_sub * NSA_REP * tq), lambda b, h, i: (b, h, 0, i)),
        ],
        out_shape=[
            jax.ShapeDtypeStruct((bsz, g, K_PAD, NSA_REP * t), BF16),
            jax.ShapeDtypeStruct((bsz, g, HEAD_DIM, NSA_REP * t), F32),
        ],
        compiler_params=_params("parallel", "parallel", "parallel"),
        name="cmpattn",
    )(q, kc, vo)


def _overlap_t(t):
    ncp = t // CMP_STRIDE
    n_cmp = ncp - 1
    n_slc = t // SEL_BLOCK
    starts = np.arange(n_cmp) * CMP_STRIDE
    sel_starts = np.arange(n_slc) * SEL_BLOCK
    ov = ((starts[:, None] < sel_starts[None, :] + SEL_BLOCK)
          & (starts[:, None] + CMP_BLOCK > sel_starts[None, :])).astype(np.float32)
    ovp = np.zeros((ncp, SEL_PAD), np.float32)
    ovp[:n_cmp, :n_slc] = ov
    return jnp.asarray(ovp.T, dtype=BF16)


class _AttnTile:
    def __init__(self, qi, odd, qt, refs, bufs, tq):
        self.qi, self.odd = qi, odd
        self.qt, self.tq, self.tk = qt, tq, tq
        self.ks_ref, self.kw_ref, self.vs_ref, self.vw_ref, self.e_ref = refs
        self.s0_ref, self.s1_ref = bufs
        cols = NSA_REP * tq
        self.q_pos = qi * tq + lax.broadcasted_iota(jnp.int32, (1, cols), 1) % tq
        self.k_off = lax.broadcasted_iota(jnp.int32, (tq, cols), 0)
        self.ones_rows = jnp.where(lax.broadcasted_iota(jnp.int32, (BF16_ROWS, tq), 0) == 0, 1.0, 0.0).astype(BF16)
        self.init = (jnp.full((1, cols), NEG, F32), jnp.zeros((V_ROWS, cols), F32))

    def tile(self, kt):
        return pl.ds(pl.multiple_of(kt * self.tk, self.tk), self.tk)

    def sel_scores(self, kt):
        return _dot(self.ks_ref[self.tile(kt), :] + self.e_ref[self.tile(kt), :], self.qt)

    def win_scores(self, kt):
        return _dot(self.kw_ref[self.tile(kt), :], self.qt)

    def update(self, carry, scores, v_ref, kt):
        m, acc = carry
        m_new = jnp.maximum(m, jnp.max(scores(), axis=0, keepdims=True))
        p = jnp.exp2(scores() - m_new).astype(BF16)
        vt = jnp.concatenate([v_ref[:, self.tile(kt)], self.ones_rows], axis=0)
        return m_new, jnp.exp2(m - m_new) * acc + _dot(vt, p)

    @staticmethod
    def normalise(carry):
        _, acc = carry
        return acc[0:HEAD_DIM, :] * (1.0 / acc[HEAD_DIM:HEAD_DIM + 1, :])

    def causal(self, kt, s):
        return jnp.where(self.k_off <= self.q_pos - kt * self.tk, s, NEG)

    def start(self):
        self.s0_ref[...] = self.sel_scores(0)

    def sel_pair(self, j, carry):
        self.s1_ref[...] = self.sel_scores(2 * j + 1)
        carry = self.update(carry, lambda: self.s0_ref[...], self.vs_ref, 2 * j)
        self.s0_ref[...] = self.sel_scores(2 * j + 2)
        return self.update(carry, lambda: self.s1_ref[...], self.vs_ref, 2 * j + 1)

    def tail(self):
        qi, tk = self.qi, self.tk
        n_back = WINDOW // tk
        assert n_back % 2 == 0
        even = 2 * (qi // 2)

        def win_masked(back):
            kt = jnp.maximum(qi - back, 0)
            s = self.win_scores(kt)
            if back == 0:
                return kt, self.causal(qi, s)
            if back == n_back:
                oldest_visible = jnp.where(qi >= back, self.q_pos - kt * tk - WINDOW, tk)
                return kt, jnp.where(self.k_off > oldest_visible, s, NEG)
            return kt, jnp.where(qi >= back, s, NEG)

        win_carry = self.init
        pending = None
        for n, back in enumerate(range(n_back, -1, -1)):
            buf = (self.s1_ref, self.s0_ref)[n % 2]
            kt, s = win_masked(back)
            buf[...] = s
            if pending is None:
                held = (lambda: self.s0_ref[...]) if self.odd else (lambda: self.causal(even, self.s0_ref[...]))
                self.sel_carry = self.update(self.sel_carry, held, self.vs_ref, even)
            else:
                win_carry = self.update(win_carry, lambda b=pending[0]: b[...], self.vw_ref, pending[1])
            pending = (buf, kt)
        if self.odd:
            self.s0_ref[...] = self.causal(qi, self.sel_scores(qi))
        self.o_win = self.normalise(self.update(win_carry, lambda: pending[0][...], self.vw_ref, pending[1]))
        if self.odd:
            self.sel_carry = self.update(self.sel_carry, lambda: self.s0_ref[...], self.vs_ref, qi)
        self.o_sel = self.normalise(self.sel_carry)


def _attn_kernel(qt_ref, ks_ref, kw_ref, vs_ref, vw_ref, e_ref, oc_ref, gl_ref, o_ref, *bufs, tq, n_sub):
    cols = NSA_REP * tq
    refs = (ks_ref, kw_ref, vs_ref, vw_ref, e_ref)
    assert n_sub % 2 == 0
    step = pl.program_id(2)
    tiles = [_AttnTile(step * n_sub + sub, sub % 2 == 1, qt_ref[:, sub * cols:(sub + 1) * cols], refs,
                       bufs[2 * sub:2 * sub + 2], tq) for sub in range(n_sub)]
    for tl in tiles:
        tl.start()
    common = step * (n_sub // 2)
    joint = lambda j, carries: tuple(tl.sel_pair(j, c) for tl, c in zip(tiles, carries))
    carries = lax.fori_loop(0, common, joint, tuple(tl.init for tl in tiles))
    for sub, (tl, carry) in enumerate(zip(tiles, carries)):
        for k in range(sub // 2):
            carry = tl.sel_pair(common + k, carry)
        tl.sel_carry = carry
    for tl in tiles:
        tl.tail()

    for sub, tl in enumerate(tiles):
        gates = jax.nn.sigmoid(gl_ref[:, sub * tq:(sub + 1) * tq])
        oc = oc_ref[:, sub * cols:(sub + 1) * cols]
        outs = []
        for r in range(NSA_REP):
            c = slice(r * tq, (r + 1) * tq)
            outs.append(gates[r:r + 1, :] * oc[:, c]
                        + gates[NSA_REP + r:NSA_REP + r + 1, :] * tl.o_sel[:, c]
                        + gates[2 * NSA_REP + r:2 * NSA_REP + r + 1, :] * tl.o_win[:, c])
        o_ref[sub * tq:(sub + 1) * tq, :] = jnp.concatenate(outs, axis=0).T.astype(o_ref.dtype)


def _attn(qt, ks, kw, vst, vwt, e, oc, glt):
    bsz, t, _ = ks.shape
    g = NSA_KV_HEADS
    tq = T_ATTN
    cols = NSA_REP * tq
    full_k = pl.BlockSpec((None, t, K_PAD), lambda b, h, i: (b, 0, h))
    full_v = pl.BlockSpec((None, HEAD_DIM, t), lambda b, h, i: (b, h, 0))
    n_sub = ATTN_SUBTILES
    return pl.pallas_call(
        functools.partial(_attn_kernel, tq=tq, n_sub=n_sub),
        grid=(bsz, g, t // (n_sub * tq)),
        in_specs=[
            pl.BlockSpec((None, None, K_PAD, n_sub * cols), lambda b, h, i: (b, h, 0, i)),
            full_k, full_k, full_v, full_v,
            _const_spec(e.shape),
            pl.BlockSpec((None, None, HEAD_DIM, n_sub * cols), lambda b, h, i: (b, h, 0, i)),
            pl.BlockSpec((None, None, 4 * NSA_REP, n_sub * tq), lambda b, h, i: (b, h, 0, i)),
        ],
        out_specs=pl.BlockSpec((None, n_sub * tq, NSA_REP * HEAD_DIM), lambda b, h, i: (b, i, h)),
        out_shape=jax.ShapeDtypeStruct((bsz, t, NSA_WIDTH), BF16),
        scratch_shapes=[pltpu.VMEM((tq, cols), F32)] * (2 * n_sub),
        compiler_params=_params("parallel", "parallel", "arbitrary"),
        name="attn",
    )(qt, ks, kw, vst, vwt, e, oc, glt)


def _mix_kernel(x_ref, ys_ref, yn_ref, g_ref, wm_ref, wbs_ref, wbn_ref, wo_ref, h_ref):
    x = x_ref[...]
    n1 = _rms(x, g_ref[...]).astype(BF16)
    logits = _dot(n1, wm_ref[...])
    a = _dot(ys_ref[...], wbs_ref[...])
    b = _dot(yn_ref[...], wbn_ref[...])
    mixed = jax.nn.sigmoid(logits[:, :D_MODEL]) * a + jax.nn.sigmoid(logits[:, D_MODEL:]) * b
    h_ref[...] = x + _dot(mixed.astype(BF16), wo_ref[...])


def _mix(x, ys, yn, g, wm, wbs, wbn, wo):
    bsz, t, d = x.shape
    tm = TM_MIX
    row = lambda w: pl.BlockSpec((None, tm, w), lambda b, i: (b, i, 0))
    return pl.pallas_call(
        _mix_kernel,
        grid=(bsz, t // tm),
        in_specs=[row(d), row(SSM_WIDTH), row(NSA_WIDTH), _const_spec(g.shape), _const_spec(wm.shape),
                  _const_spec(wbs.shape), _const_spec(wbn.shape), _const_spec(wo.shape)],
        out_specs=row(d),
        out_shape=jax.ShapeDtypeStruct((bsz, t, d), F32),
        compiler_params=_params("parallel", "parallel"),
        name="mix",
    )(x, ys, yn, g, wm, wbs, wbn, wo)


def _ffn_kernel(h_ref, p_ref, gf_ref, wu_ref, cw_ref, cbias_ref, wd_ref,
                gp_ref, wg_ref, wp_ref, gl_ref, o_ref, ca_ref, cb_ref, act_ref, *, tm):
    first_tile = pl.program_id(1) == 0
    h = h_ref[...]
    n2 = _rms(h, gf_ref[...]).astype(BF16)

    def cols(c, half):
        return slice(half * D_FF + c * FF_CHUNK, half * D_FF + (c + 1) * FF_CHUNK)

    def conv(hid, buf_ref, c, cw, cb):
        buf_ref[c, 0:SUBLANES, :] = buf_ref[c, tm:tm + SUBLANES, :]
        buf_ref[c, SUBLANES:SUBLANES + tm, :] = hid
        x1 = buf_ref[c, pl.ds(SUBLANES - 1, tm), :]
        x2 = buf_ref[c, pl.ds(SUBLANES - 2, tm), :]
        return cw[0:1, :] * x2 + cw[1:2, :] * x1 + cw[2:3, :] * hid + cb

    @pl.when(first_tile)
    def _():
        ca_ref[:, tm:tm + SUBLANES, :] = jnp.zeros((N_FF_CHUNKS, SUBLANES, FF_CHUNK), F32)
        cb_ref[:, tm:tm + SUBLANES, :] = jnp.zeros((N_FF_CHUNKS, SUBLANES, FF_CHUNK), F32)

    up = lambda c: (_dot(n2, wu_ref[:, cols(c, 0)]), _dot(n2, wu_ref[:, cols(c, 1)]))
    nxt = up(0)
    acc = None
    for c in range(N_FF_CHUNKS):
        ha, hb = nxt
        if c + 1 < N_FF_CHUNKS:
            nxt = up(c + 1)
        a = conv(ha, ca_ref, c, cw_ref[:, cols(c, 0)], cbias_ref[:, cols(c, 0)])
        b = conv(hb, cb_ref, c, cw_ref[:, cols(c, 1)], cbias_ref[:, cols(c, 1)])
        act_ref[:, c * FF_CHUNK:(c + 1) * FF_CHUNK] = (jax.nn.gelu(a) * b).astype(BF16)
        if (c + 1) % FF_DOWN_GROUP == 0 or c + 1 == N_FF_CHUNKS:
            lo = (c // FF_DOWN_GROUP) * FF_DOWN_GROUP * FF_CHUNK
            down = _dot(act_ref[:, lo:(c + 1) * FF_CHUNK], wd_ref[lo:(c + 1) * FF_CHUNK, :])
            acc = down if acc is None else acc + down
    h2 = h + acc
    gate = jax.nn.sigmoid(_dot(_rms(h2, gp_ref[...]).astype(BF16), wg_ref[...]))
    h3 = h2 + gate * _dot(p_ref[...].astype(BF16), wp_ref[...])
    o_ref[...] = _rms(h3, gl_ref[...])


def _ffn(h, p, gf, w_up, conv_w, conv_b, w_down, gp, wg, wp, gl):
    bsz, t, d = h.shape
    tm = TM_FFN
    nc, ch = N_FF_CHUNKS, FF_CHUNK

    row = lambda w: pl.BlockSpec((None, tm, w), lambda b, i: (b, i, 0))
    consts = [gf, w_up.astype(BF16), conv_w, conv_b.reshape(1, 2 * D_FF), w_down.astype(BF16), gp, wg, wp, gl]
    return pl.pallas_call(
        functools.partial(_ffn_kernel, tm=tm),
        grid=(bsz, t // tm),
        in_specs=[row(d), row(PLE_DIM)] + [_const_spec(c.shape) for c in consts],
        out_specs=row(d),
        out_shape=jax.ShapeDtypeStruct((bsz, t, d), F32),
        scratch_shapes=[
            pltpu.VMEM((nc, SUBLANES + tm, ch), F32),
            pltpu.VMEM((nc, SUBLANES + tm, ch), F32),
            pltpu.VMEM((tm, D_FF), BF16),
        ],
        compiler_params=_params("arbitrary", "arbitrary"),
        name="ffn",
    )(h, p, *consts)


def _layer(x, p, g_mix, w_in, ssm_a_re, ssm_a_im, ssm_log_dt, ssm_b_re, ssm_b_im, ssm_c_re, ssm_c_im,
           ssm_d, ssm_w_glu, cmp_pe_k, cmp_pe_v, cmp_wk1, cmp_wk2, cmp_wv1, cmp_wv2, w_br_ssm, w_br_nsa,
           w_out, g_ffn, w_up, conv_w, conv_b, w_down, g_ple, w_ple_gate, w_ple_proj, g_out):
    bsz, t, d = x.shape
    g = NSA_KV_HEADS
    assert d == D_MODEL and t % (ATTN_SUBTILES * T_ATTN) == 0 and t // SEL_BLOCK <= SEL_PAD

    u, q, kvc, ks, kw, vs, vw, gate_logits = _inproj(x, g_mix.reshape(1, d), _inproj_weight(w_in))

    wb, wc, ar, ai = _ssm_weights(ssm_a_re, ssm_a_im, ssm_log_dt, ssm_b_re, ssm_b_im, ssm_c_re, ssm_c_im)
    y_ssm = _ssm(u, wb, wc, ar, ai, ssm_d.reshape(1, SSM_WIDTH), ssm_w_glu.astype(BF16))

    kc = _compress(kvc, 0, cmp_pe_k, cmp_wk1, cmp_wk2, transpose_out=False)
    vct = _compress(kvc, 1, cmp_pe_v, cmp_wv1, cmp_wv2, transpose_out=True)
    vo = jnp.concatenate([vct, jnp.broadcast_to(_overlap_t(t), (bsz, g, SEL_PAD, t // CMP_STRIDE))], axis=2)
    qt, o_cmp = _cmpattn(q, kc, vo)

    blk_of_key = np.arange(t)[:, None] // SEL_BLOCK == np.arange(SEL_PAD)[None, :]
    sel_cols = np.concatenate([np.zeros((t, HEAD_DIM)), np.where(blk_of_key, SEL_BIAS, 0.0)], axis=1)
    glt = gate_logits[:, :, :3 * NSA_HEADS].reshape(bsz, t, 3, g, NSA_REP).transpose(0, 3, 2, 4, 1)
    glt = jnp.pad(glt.reshape(bsz, g, 3 * NSA_REP, t), ((0, 0), (0, 0), (0, NSA_REP), (0, 0)))
    y_nsa = _attn(qt, ks, kw, vs.transpose(0, 2, 1), vw.transpose(0, 2, 1), jnp.asarray(sel_cols, dtype=BF16),
                  o_cmp, glt)

    h = _mix(x, y_ssm, y_nsa, g_mix.reshape(1, d), w_in[:, OFF_MERGE:].astype(BF16), w_br_ssm.astype(BF16),
             w_br_nsa.astype(BF16), w_out.astype(BF16))
    return _ffn(h, p, g_ffn.reshape(1, d), w_up, conv_w, conv_b, w_down, g_ple.reshape(1, d),
                w_ple_gate.astype(BF16), w_ple_proj.astype(BF16), g_out.reshape(1, d))


def kernel(x, p, g_mix, w_in, ssm_a_re, ssm_a_im, ssm_log_dt, ssm_b_re, ssm_b_im, ssm_c_re, ssm_c_im, ssm_d, ssm_w_glu, cmp_pe_k, cmp_pe_v, cmp_wk1, cmp_wk2, cmp_wv1, cmp_wv2, w_br_ssm, w_br_nsa, w_out, g_ffn, w_up, conv_w, conv_b, w_down, g_ple, w_ple_gate, w_ple_proj, g_final):
    assert p.shape[0] == 1, "one trunk layer"
    return _layer(x, p[0], g_mix[0], w_in[0], ssm_a_re[0], ssm_a_im[0], ssm_log_dt[0], ssm_b_re[0], ssm_b_im[0],
                  ssm_c_re[0], ssm_c_im[0], ssm_d[0], ssm_w_glu[0], cmp_pe_k[0], cmp_pe_v[0], cmp_wk1[0],
                  cmp_wk2[0], cmp_wv1[0], cmp_wv2[0], w_br_ssm[0], w_br_nsa[0], w_out[0], g_ffn[0], w_up[0],
                  conv_w[0], conv_b[0], w_down[0], g_ple[0], w_ple_gate[0], w_ple_proj[0], g_final)
```

```python
import functools

import numpy as np
import jax
import jax.numpy as jnp
from jax import lax
from jax.experimental import pallas as pl
from jax.experimental.pallas import tpu as pltpu

F32 = jnp.float32
BF16 = jnp.bfloat16

D_MODEL = 1024
SSM_WIDTH = 512
SSM_GROUP = 16
SSM_GROUPS = SSM_WIDTH // SSM_GROUP
SSM_STATE = 64
NSA_HEADS = 8
NSA_KV_HEADS = 2
NSA_REP = NSA_HEADS // NSA_KV_HEADS
HEAD_DIM = 64
NSA_WIDTH = NSA_HEADS * HEAD_DIM
KV_WIDTH = NSA_KV_HEADS * HEAD_DIM
CMP_BLOCK = 32
CMP_STRIDE = 16
CMP_HIDDEN = 128
SEL_BLOCK = 64
N_SELECT = 16
WINDOW = 512
D_FF = 2816
PLE_DIM = 256
EPS = 1e-6
NEG = -1e30
FORCE_BONUS = 100.0

OFF_Q = SSM_WIDTH
OFF_KV = OFF_Q + NSA_WIDTH
OFF_NSA_GATE = OFF_KV + 6 * KV_WIDTH
OFF_MERGE = OFF_NSA_GATE + 3 * NSA_HEADS

LANES = 128
SUBLANES = 8
BF16_ROWS = 16
VMEM_LIMIT = 56 * 1024 * 1024
GATE_PAD = LANES
SEL_PAD = 64
K_PAD = HEAD_DIM + SEL_PAD
V_ROWS = HEAD_DIM + BF16_ROWS
SEL_BIAS = 2.0 ** 100
LOG2E = 1.4426950408889634
SSM_BLOCKS = SSM_WIDTH // LANES
SSM_BLOCK_STATE = (LANES // SSM_GROUP) * SSM_STATE
SSM_SLABS = 2 * SSM_BLOCK_STATE // LANES
FF_CHUNK = 256
N_FF_CHUNKS = D_FF // FF_CHUNK
FF_DOWN_GROUP = 4

TM_INPROJ = 512
TM_MIX = 1024
TM_FFN = 512
T_SSM = 64
T_ATTN = 2 * LANES
ATTN_SUBTILES = 4


def _rms(x, g):
    return x * lax.rsqrt(jnp.mean(x * x, axis=-1, keepdims=True) + EPS) * g


def _dot(a, b):
    return jnp.dot(a, b, preferred_element_type=F32)


def _params(*sem):
    return pltpu.CompilerParams(dimension_semantics=sem, vmem_limit_bytes=VMEM_LIMIT)


def _const_spec(shape):
    zeros = (0,) * len(shape)
    return pl.BlockSpec(shape, lambda *_: zeros, pipeline_mode=pl.Buffered(1))


_INPROJ_WIDTHS = (SSM_WIDTH, NSA_WIDTH, 2 * KV_WIDTH, NSA_KV_HEADS * K_PAD, NSA_KV_HEADS * K_PAD,
                  KV_WIDTH, KV_WIDTH, GATE_PAD)
_INPROJ_DTYPES = (F32, F32, F32, BF16, BF16, BF16, BF16, F32)


def _inproj_kernel(x_ref, g_ref, w_ref, *out_refs):
    n1 = _rms(x_ref[...], g_ref[...]).astype(BF16)
    z = _dot(n1, w_ref[...])
    off = 0
    for ref, width in zip(out_refs, _INPROJ_WIDTHS):
        ref[...] = z[:, off:off + width].astype(ref.dtype)
        off += width


def _inproj_weight(w_in):
    d = w_in.shape[0]
    wkv = w_in[:, OFF_KV:OFF_NSA_GATE].reshape(d, 6, NSA_KV_HEADS, HEAD_DIM)
    flat = lambda w: w.reshape(d, KV_WIDTH)
    padk = lambda w: jnp.pad(w, ((0, 0), (0, 0), (0, SEL_PAD))).reshape(d, NSA_KV_HEADS * K_PAD)
    gates = jnp.pad(w_in[:, OFF_NSA_GATE:OFF_MERGE], ((0, 0), (0, GATE_PAD - 3 * NSA_HEADS)))
    return jnp.concatenate([w_in[:, :OFF_KV], flat(wkv[:, 0]), flat(wkv[:, 1]), padk(wkv[:, 2]), padk(wkv[:, 4]),
                            flat(wkv[:, 3]), flat(wkv[:, 5]), gates], axis=1).astype(BF16)


def _inproj(x, g, w):
    bsz, t, d = x.shape
    tm = TM_INPROJ
    row = lambda width: pl.BlockSpec((None, tm, width), lambda b, i: (b, i, 0))
    return pl.pallas_call(
        _inproj_kernel,
        grid=(bsz, t // tm),
        in_specs=[row(d), _const_spec((1, d)), _const_spec(w.shape)],
        out_specs=[row(width) for width in _INPROJ_WIDTHS],
        out_shape=[jax.ShapeDtypeStruct((bsz, t, width), dt) for width, dt in zip(_INPROJ_WIDTHS, _INPROJ_DTYPES)],
        compiler_params=_params("parallel", "parallel"),
        name="inproj",
    )(x, g, w)


def _ssm_kernel(u_ref, wb_ref, wc_ref, ar_ref, ai_ref, d_ref, wglu_ref, o_ref, bu_ref, xs_ref, yt_ref, st_ref,
                *, tc, nb, pitch):
    @pl.when(pl.program_id(0) == 0)
    def _():
        st_ref[...] = jnp.zeros_like(st_ref)

    u = u_ref[...].reshape(nb * tc, SSM_WIDTH)
    ub = u.astype(BF16)
    half = SSM_SLABS // 2
    for cb in range(SSM_BLOCKS):
        res = _dot(ub[:, cb * LANES:(cb + 1) * LANES], wb_ref[cb])
        for k in range(SSM_SLABS):
            for b in range(nb):
                bu_ref[cb * SSM_SLABS + k, b * pitch:b * pitch + tc, :] = res[b * tc:(b + 1) * tc, k * LANES:(k + 1) * LANES]

    for cb in range(SSM_BLOCKS):
        w2 = 2 * SSM_BLOCK_STATE
        re_cols = slice(cb * w2, cb * w2 + SSM_BLOCK_STATE)
        im_cols = slice(cb * w2 + SSM_BLOCK_STATE, (cb + 1) * w2)
        ar = jnp.broadcast_to(ar_ref[cb], (nb, SSM_BLOCK_STATE))
        ai = jnp.broadcast_to(ai_ref[cb], (nb, SSM_BLOCK_STATE))
        re_slabs = [cb * SSM_SLABS + k for k in range(half)]
        im_slabs = [cb * SSM_SLABS + half + k for k in range(half)]

        def step(t, carry, ar=ar, ai=ai, re_slabs=re_slabs, im_slabs=im_slabs):
            sr, si = carry
            rows = pl.ds(t, nb, stride=pitch)
            br = jnp.concatenate([bu_ref[s, rows, :] for s in re_slabs], axis=1)
            bi = jnp.concatenate([bu_ref[s, rows, :] for s in im_slabs], axis=1)
            nr = ar * sr - ai * si + br
            ni = ar * si + ai * sr + bi
            out_rows = pl.ds(pl.multiple_of(t * nb, nb), nb)
            for k in range(half):
                xs_ref[re_slabs[k], out_rows, :] = nr[:, k * LANES:(k + 1) * LANES]
                xs_ref[im_slabs[k], out_rows, :] = ni[:, k * LANES:(k + 1) * LANES]
            return nr, ni

        sr, si = lax.fori_loop(0, tc, step, (st_ref[:, re_cols], st_ref[:, im_cols]), unroll=True)
        st_ref[:, re_cols] = sr
        st_ref[:, im_cols] = si
        states = jnp.concatenate([xs_ref[cb * SSM_SLABS + k] for k in range(SSM_SLABS)], axis=1)
        yt_ref[cb] = _dot(states.astype(BF16), wc_ref[cb])

    cx = jnp.concatenate(
        [jnp.concatenate([yt_ref[cb, pl.ds(b, tc, stride=nb), :] for b in range(nb)], axis=0)
         for cb in range(SSM_BLOCKS)], axis=1)
    y = jax.nn.gelu(cx + d_ref[...] * u)
    gate = _dot(y.astype(BF16), wglu_ref[...])
    o_ref[...] = (y * jax.nn.sigmoid(gate)).astype(o_ref.dtype).reshape(nb, tc, SSM_WIDTH)


def _ssm_weights(a_re, a_im, log_dt, b_re, b_im, c_re, c_im):
    gpb = LANES // SSM_GROUP
    dt = jnp.exp(log_dt)[:, None]
    lr = jnp.minimum(a_re, -1e-4)
    li = a_im
    mag = jnp.exp(lr * dt)
    abr = mag * jnp.cos(li * dt)
    abi = mag * jnp.sin(li * dt)
    den = lr * lr + li * li
    qr = ((abr - 1.0) * lr + abi * li) / den
    qi = (abi * lr - (abr - 1.0) * li) / den
    bbr = b_re * qr[:, :, None] - b_im * qi[:, :, None]
    bbi = b_re * qi[:, :, None] + b_im * qr[:, :, None]
    eye = jnp.eye(gpb, dtype=F32)

    def pack_b(m):
        m4 = m.reshape(SSM_BLOCKS, gpb, SSM_STATE, SSM_GROUP)
        return jnp.einsum("kgpc,gh->kgchp", m4, eye).reshape(SSM_BLOCKS, LANES, SSM_BLOCK_STATE)

    def pack_c(m):
        m4 = m.reshape(SSM_BLOCKS, gpb, SSM_GROUP, SSM_STATE)
        return jnp.einsum("kgcp,gh->kgphc", m4, eye).reshape(SSM_BLOCKS, SSM_BLOCK_STATE, LANES)

    wb = jnp.concatenate([pack_b(bbr), pack_b(bbi)], axis=2).astype(BF16)
    wc = jnp.concatenate([pack_c(c_re), -pack_c(c_im)], axis=1).astype(BF16)
    ar = abr.reshape(SSM_BLOCKS, 1, SSM_BLOCK_STATE)
    ai = abi.reshape(SSM_BLOCKS, 1, SSM_BLOCK_STATE)
    return wb, wc, ar, ai


def _ssm(u, wb, wc, ar, ai, d, wglu):
    bsz, t, _ = u.shape
    tc = T_SSM
    pitch = tc + SUBLANES
    n_slabs = SSM_BLOCKS * SSM_SLABS
    blk = pl.BlockSpec((bsz, tc, SSM_WIDTH), lambda i: (0, i, 0))
    return pl.pallas_call(
        functools.partial(_ssm_kernel, tc=tc, nb=bsz, pitch=pitch),
        grid=(t // tc,),
        in_specs=[blk, _const_spec(wb.shape), _const_spec(wc.shape), _const_spec(ar.shape), _const_spec(ai.shape),
                  _const_spec(d.shape), _const_spec(wglu.shape)],
        out_specs=blk,
        out_shape=jax.ShapeDtypeStruct((bsz, t, SSM_WIDTH), BF16),
        scratch_shapes=[
            pltpu.VMEM((n_slabs, bsz * pitch, LANES), F32),
            pltpu.VMEM((n_slabs, tc * bsz, LANES), F32),
            pltpu.VMEM((SSM_BLOCKS, tc * bsz, LANES), F32),
            pltpu.VMEM((bsz, n_slabs * LANES), F32),
        ],
        compiler_params=_params("arbitrary"),
        name="ssm",
    )(u, wb, wc, ar, ai, d, wglu)


def _cmp_kernel(x_ref, pe_ref, w1_ref, w2_ref, o_ref, *, transpose_out):
    n = x_ref.shape[0] // CMP_STRIDE
    first = second = None
    for j in range(CMP_STRIDE):
        xj = x_ref[pl.ds(j, n, stride=CMP_STRIDE), :]
        a = _dot((xj + pe_ref[j:j + 1, :]).astype(BF16), w1_ref[0, j])
        b = _dot((xj + pe_ref[CMP_STRIDE + j:CMP_STRIDE + j + 1, :]).astype(BF16), w1_ref[1, j])
        first = a if first is None else first + a
        second = b if second is None else second + b
    hid = first + pltpu.roll(second, n - 1, 0)
    hb = jax.nn.gelu(hid).astype(BF16)
    for g in range(NSA_KV_HEADS):
        hg = hb[:, g * CMP_HIDDEN:(g + 1) * CMP_HIDDEN]
        if transpose_out:
            out = lax.dot_general(w2_ref[...], hg, (((1,), (1,)), ((), ())), preferred_element_type=F32)
        else:
            out = _dot(hg, w2_ref[...])
        o_ref[g] = out.astype(o_ref.dtype)


def _compress(kvc, which, pe, w1, w2, transpose_out):
    bsz, t, _ = kvc.shape
    n = t // CMP_STRIDE
    g = NSA_KV_HEADS
    pe2 = jnp.tile(pe, (1, g))
    w1r = w1.reshape(2, CMP_STRIDE, HEAD_DIM, CMP_HIDDEN)
    w1bd = jnp.einsum("hjdc,gk->hjgdkc", w1r, jnp.eye(g, dtype=F32)).reshape(2, CMP_STRIDE, g * HEAD_DIM, g * CMP_HIDDEN)
    w1bd = w1bd.astype(BF16)
    if transpose_out:
        w2k = w2.T.astype(BF16)
        out_block, out_shape = (None, g, HEAD_DIM, n), (bsz, g, HEAD_DIM, n)
    else:
        w2k = w2.astype(BF16)
        out_block, out_shape = (None, g, n, HEAD_DIM), (bsz, g, n, HEAD_DIM)
    return pl.pallas_call(
        functools.partial(_cmp_kernel, transpose_out=transpose_out),
        grid=(bsz,),
        in_specs=[
            pl.BlockSpec((None, t, KV_WIDTH), lambda b: (b, 0, which)),
            _const_spec(pe2.shape),
            _const_spec(w1bd.shape),
            _const_spec(w2k.shape),
        ],
        out_specs=pl.BlockSpec(out_block, lambda b: (b, 0, 0, 0)),
        out_shape=jax.ShapeDtypeStruct(out_shape, BF16),
        compiler_params=_params("parallel"),
        name="compress_v" if transpose_out else "compress_k",
    )(kvc, pe2, w1bd, w2k)


def _cmpattn_kernel(q_ref, kc_ref, vo_ref, qt_ref, oc_ref, *, tq, n_sub, n_slc, n_sel):
    cols = NSA_REP * tq
    kc = kc_ref[...]
    ncp = kc.shape[0]
    n_grp = SEL_PAD // SUBLANES
    row_in_grp = lax.broadcasted_iota(jnp.int32, (SUBLANES, tq), 0)

    def scores_of(sub):
        t0 = (pl.program_id(2) * n_sub + sub) * tq
        c0 = sub * cols
        qt = (q_ref[sub * tq:(sub + 1) * tq, :] * (HEAD_DIM ** -0.5 * LOG2E)).T.astype(BF16)
        qcat = jnp.concatenate([qt[r * HEAD_DIM:(r + 1) * HEAD_DIM, :] for r in range(NSA_REP)], axis=1)
        qt_ref[0:HEAD_DIM, c0:c0 + cols] = qcat
        t_idx = t0 + lax.broadcasted_iota(jnp.int32, (1, cols), 1) % tq
        last_visible = jnp.floor_divide(t_idx - (CMP_BLOCK - 1), CMP_STRIDE)
        cmask = lax.broadcasted_iota(jnp.int32, (ncp, cols), 0) <= last_visible
        s = jnp.where(cmask, _dot(kc, qcat), NEG)
        e = jnp.exp2(s - jnp.max(s, axis=0, keepdims=True))
        p = jnp.where(cmask, e * (1.0 / jnp.sum(e, axis=0, keepdims=True)), 0.0).astype(BF16)
        both = _dot(vo_ref[...], p)
        oc_ref[:, c0:c0 + cols] = both[0:HEAD_DIM, :]
        imp = both[HEAD_DIM:, 0:tq]
        for r in range(1, NSA_REP):
            imp = imp + both[HEAD_DIM:, r * tq:(r + 1) * tq]
        j_idx = lax.broadcasted_iota(jnp.int32, (SEL_PAD, tq), 0)
        cur = (t0 + lax.broadcasted_iota(jnp.int32, (SEL_PAD, tq), 1)) // SEL_BLOCK
        valid = j_idx <= cur
        forced = (j_idx == 0) | (j_idx == cur) | (j_idx == cur - 1)
        return jnp.where(valid, imp, -1.0) + jnp.where(forced, FORCE_BONUS, 0.0), valid

    def count_group(rank, score, jg):
        rank = list(rank)
        for j in range(jg * SUBLANES, min((jg + 1) * SUBLANES, n_slc)):
            row = jnp.broadcast_to(score[j:j + 1, :], (SUBLANES, tq))
            for g in range(n_grp):
                grp = score[SUBLANES * g:SUBLANES * (g + 1), :]
                if g < jg:
                    beats = row > grp
                elif g > jg:
                    beats = row >= grp
                else:
                    beats = (row > grp) | ((row == grp) & (row_in_grp > j % SUBLANES))
                rank[g] = rank[g] + jnp.where(beats, 1.0, 0.0)
        return tuple(rank)

    tiles = [scores_of(sub) for sub in range(n_sub)]
    zeros = tuple(jnp.zeros((SUBLANES, tq), F32) for _ in range(n_grp))
    ranks = [count_group(zeros, score, 0) for score, _ in tiles]
    for sub, (score, valid) in enumerate(tiles):
        t0 = (pl.program_id(2) * n_sub + sub) * tq
        last_grp = (t0 + tq - 1) // SEL_BLOCK // SUBLANES
        rank = ranks[sub]
        for jg in range(1, pl.cdiv(n_slc, SUBLANES)):
            rank = lax.cond(jg <= last_grp, functools.partial(count_group, score=score, jg=jg), lambda r: r, rank)
        sel = (jnp.concatenate(rank, axis=0) < n_sel) & valid
        selm1 = jnp.where(sel, 0.0, -1.0).astype(BF16)
        for r in range(NSA_REP):
            qt_ref[HEAD_DIM:K_PAD, sub * cols + r * tq:sub * cols + (r + 1) * tq] = selm1


def _overlap_t(t):
    ncp = t // CMP_STRIDE
    n_cmp = ncp - 1
    n_slc = t // SEL_BLOCK
    starts = np.arange(n_cmp) * CMP_STRIDE
    sel_starts = np.arange(n_slc) * SEL_BLOCK
    ov = ((starts[:, None] < sel_starts[None, :] + SEL_BLOCK)
          & (starts[:, None] + CMP_BLOCK > sel_starts[None, :])).astype(np.float32)
    ovp = np.zeros((ncp, SEL_PAD), np.float32)
    ovp[:n_cmp, :n_slc] = ov
    return jnp.asarray(ovp.T, dtype=BF16)


class _AttnTile:
    def __init__(self, qi, odd, qt, refs, bufs, tq):
        self.qi, self.odd = qi, odd
        self.qt, self.tq, self.tk = qt, tq, tq
        self.ks_ref, self.kw_ref, self.vs_ref, self.vw_ref, self.e_ref = refs
        self.s0_ref, self.s1_ref = bufs
        cols = NSA_REP * tq
        self.q_pos = qi * tq + lax.broadcasted_iota(jnp.int32, (1, cols), 1) % tq
        self.k_off = lax.broadcasted_iota(jnp.int32, (tq, cols), 0)
        self.ones_rows = jnp.where(lax.broadcasted_iota(jnp.int32, (BF16_ROWS, tq), 0) == 0, 1.0, 0.0).astype(BF16)
        self.init = (jnp.full((1, cols), NEG, F32), jnp.zeros((V_ROWS, cols), F32))

    def tile(self, kt):
        return pl.ds(pl.multiple_of(kt * self.tk, self.tk), self.tk)

    def sel_scores(self, kt):
        return _dot(self.ks_ref[self.tile(kt), :] + self.e_ref[self.tile(kt), :], self.qt)

    def win_scores(self, kt):
        return _dot(self.kw_ref[self.tile(kt), :], self.qt)

    def update(self, carry, scores, v_ref, kt):
        m, acc = carry
        m_new = jnp.maximum(m, jnp.max(scores(), axis=0, keepdims=True))
        p = jnp.exp2(scores() - m_new).astype(BF16)
        vt = jnp.concatenate([v_ref[:, self.tile(kt)], self.ones_rows], axis=0)
        return m_new, jnp.exp2(m - m_new) * acc + _dot(vt, p)

    @staticmethod
    def normalise(carry):
        _, acc = carry
        return acc[0:HEAD_DIM, :] * (1.0 / acc[HEAD_DIM:HEAD_DIM + 1, :])

    def causal(self, kt, s):
        return jnp.where(self.k_off <= self.q_pos - kt * self.tk, s, NEG)

    def start(self):
        self.s0_ref[...] = self.sel_scores(0)

    def sel_pair(self, j, carry):
        self.s1_ref[...] = self.sel_scores(2 * j + 1)
        carry = self.update(carry, lambda: self.s0_ref[...], self.vs_ref, 2 * j)
        self.s0_ref[...] = self.sel_scores(2 * j + 2)
        return self.update(carry, lambda: self.s1_ref[...], self.vs_ref, 2 * j + 1)

    def tail(self):
        qi, tk = self.qi, self.tk
        n_back = WINDOW // tk
        assert n_back % 2 == 0
        even = 2 * (qi // 2)

        def win_masked(back):
            kt = jnp.maximum(qi - back, 0)
            s = self.win_scores(kt)
            if back == 0:
                return kt, self.causal(qi, s)
            if back == n_back:
                oldest_visible = jnp.where(qi >= back, self.q_pos - kt * tk - WINDOW, tk)
                return kt, jnp.where(self.k_off > oldest_visible, s, NEG)
            return kt, jnp.where(qi >= back, s, NEG)

        win_carry = self.init
        pending = None
        for n, back in enumerate(range(n_back, -1, -1)):
            buf = (self.s1_ref, self.s0_ref)[n % 2]
            kt, s = win_masked(back)
            buf[...] = s
            if pending is None:
                held = (lambda: self.s0_ref[...]) if self.odd else (lambda: self.causal(even, self.s0_ref[...]))
                self.sel_carry = self.update(self.sel_carry, held, self.vs_ref, even)
            else:
                win_carry = self.update(win_carry, lambda b=pending[0]: b[...], self.vw_ref, pending[1])
            pending = (buf, kt)
        if self.odd:
            self.s0_ref[...] = self.causal(qi, self.sel_scores(qi))
        self.o_win = self.normalise(self.update(win_carry, lambda: pending[0][...], self.vw_ref, pending[1]))
        if self.odd:
            self.sel_carry = self.update(self.sel_carry, lambda: self.s0_ref[...], self.vs_ref, qi)
        self.o_sel = self.normalise(self.sel_carry)


def _attn_kernel(qt_ref, ks_ref, kw_ref, vs_ref, vw_ref, e_ref, oc_ref, gl_ref, o_ref, *bufs, tq, n_sub):
    cols = NSA_REP * tq
    refs = (ks_ref, kw_ref, vs_ref, vw_ref, e_ref)
    assert n_sub % 2 == 0
    step = pl.program_id(2)
    tiles = [_AttnTile(step * n_sub + sub, sub % 2 == 1, qt_ref[:, sub * cols:(sub + 1) * cols], refs,
                       bufs[2 * sub:2 * sub + 2], tq) for sub in range(n_sub)]
    for tl in tiles:
        tl.start()
    common = step * (n_sub // 2)
    joint = lambda j, carries: tuple(tl.sel_pair(j, c) for tl, c in zip(tiles, carries))
    carries = lax.fori_loop(0, common, joint, tuple(tl.init for tl in tiles))
    for sub, (tl, carry) in enumerate(zip(tiles, carries)):
        for k in range(sub // 2):
            carry = tl.sel_pair(common + k, carry)
        tl.sel_carry = carry
    for tl in tiles:
        tl.tail()

    for sub, tl in enumerate(tiles):
        gates = jax.nn.sigmoid(gl_ref[:, sub * tq:(sub + 1) * tq])
        oc = oc_ref[:, sub * cols:(sub + 1) * cols]
        outs = []
        for r in range(NSA_REP):
            c = slice(r * tq, (r + 1) * tq)
            outs.append(gates[r:r + 1, :] * oc[:, c]
                        + gates[NSA_REP + r:NSA_REP + r + 1, :] * tl.o_sel[:, c]
                        + gates[2 * NSA_REP + r:2 * NSA_REP + r + 1, :] * tl.o_win[:, c])
        o_ref[sub * tq:(sub + 1) * tq, :] = jnp.concatenate(outs, axis=0).T.astype(o_ref.dtype)


def _nsa_kernel(q_ref, kc_ref, vo_ref, ks_ref, kw_ref, vs_ref, vw_ref, e_ref, gl_ref, o_ref, qt_ref, oc_ref, *bufs,
                tq, n_sub, n_slc, n_sel):
    _cmpattn_kernel(q_ref, kc_ref, vo_ref, qt_ref, oc_ref, tq=tq, n_sub=n_sub, n_slc=n_slc, n_sel=n_sel)
    _attn_kernel(qt_ref, ks_ref, kw_ref, vs_ref, vw_ref, e_ref, oc_ref, gl_ref, o_ref, *bufs, tq=tq, n_sub=n_sub)


def _nsa(q, kc, vo, ks, kw, vst, vwt, e, glt):
    bsz, t, _ = ks.shape
    g = NSA_KV_HEADS
    tq = T_ATTN
    cols = NSA_REP * tq
    ncp = kc.shape[2]
    n_slc = t // SEL_BLOCK
    n_sub = ATTN_SUBTILES
    full_k = pl.BlockSpec((None, t, K_PAD), lambda b, h, i: (b, 0, h))
    full_v = pl.BlockSpec((None, HEAD_DIM, t), lambda b, h, i: (b, h, 0))
    return pl.pallas_call(
        functools.partial(_nsa_kernel, tq=tq, n_sub=n_sub, n_slc=n_slc, n_sel=min(N_SELECT, n_slc)),
        grid=(bsz, g, t // (n_sub * tq)),
        in_specs=[
            pl.BlockSpec((None, n_sub * tq, NSA_REP * HEAD_DIM), lambda b, h, i: (b, i, h)),
            pl.BlockSpec((None, None, ncp, HEAD_DIM), lambda b, h, i: (b, h, 0, 0)),
            pl.BlockSpec((None, None, HEAD_DIM + SEL_PAD, ncp), lambda b, h, i: (b, h, 0, 0)),
            full_k, full_k, full_v, full_v,
            _const_spec(e.shape),
            pl.BlockSpec((None, None, 4 * NSA_REP, n_sub * tq), lambda b, h, i: (b, h, 0, i)),
        ],
        out_specs=pl.BlockSpec((None, n_sub * tq, NSA_REP * HEAD_DIM), lambda b, h, i: (b, i, h)),
        out_shape=jax.ShapeDtypeStruct((bsz, t, NSA_WIDTH), BF16),
        scratch_shapes=[pltpu.VMEM((K_PAD, n_sub * cols), BF16),
                        pltpu.VMEM((HEAD_DIM, n_sub * cols), F32)]
                       + [pltpu.VMEM((tq, cols), F32)] * (2 * n_sub),
        compiler_params=_params("parallel", "parallel", "arbitrary"),
        name="nsa",
    )(q, kc, vo, ks, kw, vst, vwt, e, glt)


def _mix_kernel(x_ref, ys_ref, yn_ref, g_ref, wm_ref, wbs_ref, wbn_ref, wo_ref, h_ref):
    x = x_ref[...]
    n1 = _rms(x, g_ref[...]).astype(BF16)
    logits = _dot(n1, wm_ref[...])
    a = _dot(ys_ref[...], wbs_ref[...])
    b = _dot(yn_ref[...], wbn_ref[...])
    mixed = jax.nn.sigmoid(logits[:, :D_MODEL]) * a + jax.nn.sigmoid(logits[:, D_MODEL:]) * b
    h_ref[...] = x + _dot(mixed.astype(BF16), wo_ref[...])


def _mix(x, ys, yn, g, wm, wbs, wbn, wo):
    bsz, t, d = x.shape
    tm = TM_MIX
    row = lambda w: pl.BlockSpec((None, tm, w), lambda b, i: (b, i, 0))
    return pl.pallas_call(
        _mix_kernel,
        grid=(bsz, t // tm),
        in_specs=[row(d), row(SSM_WIDTH), row(NSA_WIDTH), _const_spec(g.shape), _const_spec(wm.shape),
                  _const_spec(wbs.shape), _const_spec(wbn.shape), _const_spec(wo.shape)],
        out_specs=row(d),
        out_shape=jax.ShapeDtypeStruct((bsz, t, d), F32),
        compiler_params=_params("parallel", "parallel"),
        name="mix",
    )(x, ys, yn, g, wm, wbs, wbn, wo)


def _ffn_kernel(h_ref, p_ref, gf_ref, wu_ref, cw_ref, cbias_ref, wd_ref,
                gp_ref, wg_ref, wp_ref, gl_ref, o_ref, ca_ref, cb_ref, act_ref, *, tm):
    first_tile = pl.program_id(1) == 0
    h = h_ref[...]
    n2 = _rms(h, gf_ref[...]).astype(BF16)

    def cols(c, half):
        return slice(half * D_FF + c * FF_CHUNK, half * D_FF + (c + 1) * FF_CHUNK)

    def conv(hid, buf_ref, c, cw, cb):
        buf_ref[c, 0:SUBLANES, :] = buf_ref[c, tm:tm + SUBLANES, :]
        buf_ref[c, SUBLANES:SUBLANES + tm, :] = hid
        x1 = buf_ref[c, pl.ds(SUBLANES - 1, tm), :]
        x2 = buf_ref[c, pl.ds(SUBLANES - 2, tm), :]
        return cw[0:1, :] * x2 + cw[1:2, :] * x1 + cw[2:3, :] * hid + cb

    @pl.when(first_tile)
    def _():
        ca_ref[:, tm:tm + SUBLANES, :] = jnp.zeros((N_FF_CHUNKS, SUBLANES, FF_CHUNK), F32)
        cb_ref[:, tm:tm + SUBLANES, :] = jnp.zeros((N_FF_CHUNKS, SUBLANES, FF_CHUNK), F32)

    up = lambda c: (_dot(n2, wu_ref[:, cols(c, 0)]), _dot(n2, wu_ref[:, cols(c, 1)]))
    nxt = up(0)
    acc = None
    for c in range(N_FF_CHUNKS):
        ha, hb = nxt
        if c + 1 < N_FF_CHUNKS:
            nxt = up(c + 1)
        a = conv(ha, ca_ref, c, cw_ref[:, cols(c, 0)], cbias_ref[:, cols(c, 0)])
        b = conv(hb, cb_ref, c, cw_ref[:, cols(c, 1)], cbias_ref[:, cols(c, 1)])
        act_ref[:, c * FF_CHUNK:(c + 1) * FF_CHUNK] = (jax.nn.gelu(a) * b).astype(BF16)
        if (c + 1) % FF_DOWN_GROUP == 0 or c + 1 == N_FF_CHUNKS:
            lo = (c // FF_DOWN_GROUP) * FF_DOWN_GROUP * FF_CHUNK
            down = _dot(act_ref[:, lo:(c + 1) * FF_CHUNK], wd_ref[lo:(c + 1) * FF_CHUNK, :])
            acc = down if acc is None else acc + down
    h2 = h + acc
    gate = jax.nn.sigmoid(_dot(_rms(h2, gp_ref[...]).astype(BF16), wg_ref[...]))
    h3 = h2 + gate * _dot(p_ref[...].astype(BF16), wp_ref[...])
    o_ref[...] = _rms(h3, gl_ref[...])


def _ffn(h, p, gf, w_up, conv_w, conv_b, w_down, gp, wg, wp, gl):
    bsz, t, d = h.shape
    tm = TM_FFN
    nc, ch = N_FF_CHUNKS, FF_CHUNK

    row = lambda w: pl.BlockSpec((None, tm, w), lambda b, i: (b, i, 0))
    consts = [gf, w_up.astype(BF16), conv_w, conv_b.reshape(1, 2 * D_FF), w_down.astype(BF16), gp, wg, wp, gl]
    return pl.pallas_call(
        functools.partial(_ffn_kernel, tm=tm),
        grid=(bsz, t // tm),
        in_specs=[row(d), row(PLE_DIM)] + [_const_spec(c.shape) for c in consts],
        out_specs=row(d),
        out_shape=jax.ShapeDtypeStruct((bsz, t, d), F32),
        scratch_shapes=[
            pltpu.VMEM((nc, SUBLANES + tm, ch), F32),
            pltpu.VMEM((nc, SUBLANES + tm, ch), F32),
            pltpu.VMEM((tm, D_FF), BF16),
        ],
        compiler_params=_params("arbitrary", "arbitrary"),
        name="ffn",
    )(h, p, *consts)


def _layer(x, p, g_mix, w_in, ssm_a_re, ssm_a_im, ssm_log_dt, ssm_b_re, ssm_b_im, ssm_c_re, ssm_c_im,
           ssm_d, ssm_w_glu, cmp_pe_k, cmp_pe_v, cmp_wk1, cmp_wk2, cmp_wv1, cmp_wv2, w_br_ssm, w_br_nsa,
           w_out, g_ffn, w_up, conv_w, conv_b, w_down, g_ple, w_ple_gate, w_ple_proj, g_out):
    bsz, t, d = x.shape
    g = NSA_KV_HEADS
    assert d == D_MODEL and t % (ATTN_SUBTILES * T_ATTN) == 0 and t // SEL_BLOCK <= SEL_PAD

    u, q, kvc, ks, kw, vs, vw, gate_logits = _inproj(x, g_mix.reshape(1, d), _inproj_weight(w_in))

    wb, wc, ar, ai = _ssm_weights(ssm_a_re, ssm_a_im, ssm_log_dt, ssm_b_re, ssm_b_im, ssm_c_re, ssm_c_im)
    y_ssm = _ssm(u, wb, wc, ar, ai, ssm_d.reshape(1, SSM_WIDTH), ssm_w_glu.astype(BF16))

    kc = _compress(kvc, 0, cmp_pe_k, cmp_wk1, cmp_wk2, transpose_out=False)
    vct = _compress(kvc, 1, cmp_pe_v, cmp_wv1, cmp_wv2, transpose_out=True)
    vo = jnp.concatenate([vct, jnp.broadcast_to(_overlap_t(t), (bsz, g, SEL_PAD, t // CMP_STRIDE))], axis=2)
    blk_of_key = np.arange(t)[:, None] // SEL_BLOCK == np.arange(SEL_PAD)[None, :]
    sel_cols = np.concatenate([np.zeros((t, HEAD_DIM)), np.where(blk_of_key, SEL_BIAS, 0.0)], axis=1)
    glt = gate_logits[:, :, :3 * NSA_HEADS].reshape(bsz, t, 3, g, NSA_REP).transpose(0, 3, 2, 4, 1)
    glt = jnp.pad(glt.reshape(bsz, g, 3 * NSA_REP, t), ((0, 0), (0, 0), (0, NSA_REP), (0, 0)))
    y_nsa = _nsa(q, kc, vo, ks, kw, vs.transpose(0, 2, 1), vw.transpose(0, 2, 1), jnp.asarray(sel_cols, dtype=BF16),
                 glt)

    h = _mix(x, y_ssm, y_nsa, g_mix.reshape(1, d), w_in[:, OFF_MERGE:].astype(BF16), w_br_ssm.astype(BF16),
             w_br_nsa.astype(BF16), w_out.astype(BF16))
    return _ffn(h, p, g_ffn.reshape(1, d), w_up, conv_w, conv_b, w_down, g_ple.reshape(1, d),
                w_ple_gate.astype(BF16), w_ple_proj.astype(BF16), g_out.reshape(1, d))


def kernel(x, p, g_mix, w_in, ssm_a_re, ssm_a_im, ssm_log_dt, ssm_b_re, ssm_b_im, ssm_c_re, ssm_c_im, ssm_d, ssm_w_glu, cmp_pe_k, cmp_pe_v, cmp_wk1, cmp_wk2, cmp_wv1, cmp_wv2, w_br_ssm, w_br_nsa, w_out, g_ffn, w_up, conv_w, conv_b, w_down, g_ple, w_ple_gate, w_ple_proj, g_final):
    assert p.shape[0] == 1, "one trunk layer"
    return _layer(x, p[0], g_mix[0], w_in[0], ssm_a_re[0], ssm_a_im[0], ssm_log_dt[0], ssm_b_re[0], ssm_b_im[0],
                  ssm_c_re[0], ssm_c_im[0], ssm_d[0], ssm_w_glu[0], cmp_pe_k[0], cmp_pe_v[0], cmp_wk1[0],
                  cmp_wk2[0], cmp_wv1[0], cmp_wv2[0], w_br_ssm[0], w_br_nsa[0], w_out[0], g_ffn[0], w_up[0],
                  conv_w[0], conv_b[0], w_down[0], g_ple[0], w_ple_gate[0], w_ple_proj[0], g_final)
```
